```python
import jax
import jax.numpy as jnp
from jax import lax
import numpy as np

D_MODEL = 1024
BATCH = 8
SEQ = 4096
DEPTH = 1

N_META = 16
CHUNK = 128
PAD = CHUNK - N_META
RET_HEADS = 4
RET_QK_DIM = 128
RET_V_DIM = 256
RET_QK = RET_HEADS * RET_QK_DIM
RET_V = RET_HEADS * RET_V_DIM
SSD_D_INNER = 2 * D_MODEL
SSD_HEAD_DIM = 64
SSD_HEADS = SSD_D_INNER // SSD_HEAD_DIM
SSD_GROUPS = 4
SSD_HPG = SSD_HEADS // SSD_GROUPS
SSD_STATE = 128
SSD_CONV = 3
SSD_XBC = SSD_D_INNER + 2 * SSD_GROUPS * SSD_STATE
D_FF = 2816
FFN_CONV = 3
EPS = 1e-6
ROPE_BASE = 10000.0
IN_SIZES = (RET_QK, RET_QK, RET_V, RET_V, SSD_D_INNER, SSD_XBC, SSD_HEADS, SSD_HEADS, D_MODEL, D_MODEL)
D_IN = sum(IN_SIZES)

kernel_name = 'hybrid_retnet_ssd_encoder_block'


def _split(t, sizes):
    out = []
    start = 0
    for s in sizes:
        out.append(t[..., start:start + s])
        start += s
    return out


def rms_norm(x, w):
    x32 = x.astype(jnp.float32)
    y = x32 * lax.rsqrt(jnp.mean(x32 * x32, axis=-1, keepdims=True) + EPS)
    return y.astype(x.dtype) * w


def group_rms_norm(x, w, groups):
    shp = x.shape
    xg = x.reshape(shp[:-1] + (groups, shp[-1] // groups)).astype(jnp.float32)
    y = xg * lax.rsqrt(jnp.mean(xg * xg, axis=-1, keepdims=True) + EPS)
    return y.reshape(shp).astype(x.dtype) * w


def head_group_norm(y):
    y32 = y.astype(jnp.float32)
    mu = jnp.mean(y32, axis=-1, keepdims=True)
    var = jnp.mean(jnp.square(y32 - mu), axis=-1, keepdims=True)
    return ((y32 - mu) * lax.rsqrt(var + EPS)).astype(y.dtype)


def dw_conv_centred(x, w, b):
    k = w.shape[0]
    y = lax.conv_general_dilated(x, w[:, None, :], window_strides=(1,), padding=[(k // 2, k // 2)],
                                 dimension_numbers=('NWC', 'WIO', 'NWC'), feature_group_count=x.shape[-1])
    return y + b


def rotary(x, pos):
    half = x.shape[-1] // 2
    inv = ROPE_BASE ** (-jnp.arange(half, dtype=jnp.float32) / half)
    ang = pos.astype(jnp.float32)[:, None] * inv[None, :]
    cos = jnp.cos(ang)[None, :, None, :].astype(x.dtype)
    sin = jnp.sin(ang)[None, :, None, :].astype(x.dtype)
    x1, x2 = x[..., :half], x[..., half:]
    return jnp.concatenate([x1 * cos - x2 * sin, x1 * sin + x2 * cos], axis=-1)


def pad_front(t):
    return jnp.pad(t, ((0, 0), (PAD, 0)) + ((0, 0),) * (t.ndim - 2))


def to_chunks(t):
    return t.reshape((t.shape[0], t.shape[1] // CHUNK, CHUNK) + t.shape[2:])


def from_chunks(t):
    return t.reshape((t.shape[0], t.shape[1] * CHUNK) + t.shape[3:])


def flip_seq(t):
    return jnp.flip(t, axis=1)


def exclusive_chunk_scan(states, decays):
    s = jnp.moveaxis(states, 1, 0)
    d = jnp.moveaxis(decays, 1, 0)

    def step(carry, sd):
        s_n, d_n = sd
        return carry * d_n + s_n, carry

    _, prev = lax.scan(step, jnp.zeros_like(s[0]), (s, d))
    return jnp.moveaxis(prev, 0, 1)


def retention_intra(qc, kc, vc, log_gamma):
    pos = jnp.arange(CHUNK, dtype=jnp.float32)
    dist = jnp.abs(pos[:, None] - pos[None, :])
    dmat = jnp.exp(log_gamma[:, None, None] * dist[None]).astype(qc.dtype)
    s = jnp.einsum('bnlhd,bnshd->bnhls', qc, kc) * dmat
    return jnp.einsum('bnhls,bnshe->bnlhe', s, vc)


def retention_cross_forward(qc, kc, vc, log_gamma):
    pos = jnp.arange(CHUNK, dtype=jnp.float32)
    k_dec = jnp.exp((CHUNK - 1 - pos)[:, None] * log_gamma[None, :]).astype(kc.dtype)
    q_dec = jnp.exp((pos + 1)[:, None] * log_gamma[None, :]).astype(qc.dtype)
    states = jnp.einsum('bnshd,sh,bnshe->bnhde', kc, k_dec, vc)
    b, n = states.shape[:2]
    chunk_dec = jnp.broadcast_to(jnp.exp(CHUNK * log_gamma).astype(states.dtype)[None, None, :, None, None],
                                 (b, n, RET_HEADS, 1, 1))
    prev = exclusive_chunk_scan(states, chunk_dec)
    return jnp.einsum('bnlhd,lh,bnhde->bnlhe', qc, q_dec, prev)


def bidirectional_retention(q, k, v, log_gamma):
    qc, kc, vc = to_chunks(q), to_chunks(k), to_chunks(v)
    y = retention_intra(qc, kc, vc, log_gamma) + retention_cross_forward(qc, kc, vc, log_gamma)
    qr, kr, vr = to_chunks(flip_seq(q)), to_chunks(flip_seq(k)), to_chunks(flip_seq(v))
    y_back = flip_seq(from_chunks(retention_cross_forward(qr, kr, vr, log_gamma)))
    return from_chunks(y) + y_back


def ssd_scan_forward(x, dt, a, bm, cm):
    b, lp = x.shape[:2]
    n = lp // CHUNK
    xd = (x * dt[..., None]).reshape(b, n, CHUNK, SSD_GROUPS, SSD_HPG, SSD_HEAD_DIM)
    acs = jnp.cumsum((dt * a).astype(jnp.float32).reshape(b, n, CHUNK, SSD_GROUPS, SSD_HPG), axis=2)
    bc = to_chunks(bm)
    cc = to_chunks(cm)
    diff = acs[:, :, :, None] - acs[:, :, None, :]
    tri = jnp.tril(jnp.ones((CHUNK, CHUNK), dtype=bool))[:, :, None, None]
    lmat = jnp.exp(jnp.where(tri, diff, -jnp.inf)).astype(x.dtype)
    cb = jnp.einsum('bclgn,bcsgn->bclsg', cc, bc)
    y_intra = jnp.einsum('bclsgh,bcsghp->bclghp', cb[..., None] * lmat, xd)
    decay_end = jnp.exp(acs[:, :, -1:] - acs).astype(x.dtype)
    states = jnp.einsum('bcsgn,bcsgh,bcsghp->bcghpn', bc, decay_end, xd)
    chunk_dec = jnp.exp(acs[:, :, -1]).astype(x.dtype)[..., None, None]
    prev = exclusive_chunk_scan(states, chunk_dec)
    y_off = jnp.einsum('bclgn,bclgh,bcghpn->bclghp', cc, jnp.exp(acs).astype(x.dtype), prev)
    return (y_intra + y_off).reshape(b, lp, SSD_HEADS, SSD_HEAD_DIM)


def hybrid_layer(h, pos, norm_mix_w, w_in, ret_gn_w, w_ret_out, w_ssd_conv, b_ssd_conv,
                 dt_bias_f, dt_bias_b, a_log_f, a_log_b, d_skip, ssd_norm_w, w_ssd_out, w_out,
                 norm_ffn_w, w_ffn_up, w_ffn_conv, b_ffn_conv, w_ffn_down):
    b, l, _ = h.shape
    u = rms_norm(h, norm_mix_w)
    proj = u @ w_in
    q, k, v, g_ret, z, xbc, dt_f, dt_b, gate_ret, gate_ssd = _split(proj, IN_SIZES)

    log_gamma = jnp.log(1.0 - 2.0 ** (-5.0 - jnp.arange(RET_HEADS, dtype=jnp.float32)))
    q = rotary(q.reshape(b, l, RET_HEADS, RET_QK_DIM), pos)
    k = rotary(k.reshape(b, l, RET_HEADS, RET_QK_DIM), pos) * (RET_QK_DIM ** -0.5)
    v = v.reshape(b, l, RET_HEADS, RET_V_DIM)
    y_ret = bidirectional_retention(pad_front(q), pad_front(k), pad_front(v), log_gamma)[:, PAD:]
    y_ret = head_group_norm(y_ret).reshape(b, l, RET_V) * ret_gn_w
    y_ret = (jax.nn.silu(g_ret) * y_ret) @ w_ret_out

    xbc = jax.nn.silu(dw_conv_centred(xbc, w_ssd_conv, b_ssd_conv))
    xs, bm, cm = _split(xbc, (SSD_D_INNER, SSD_GROUPS * SSD_STATE, SSD_GROUPS * SSD_STATE))
    xs = xs.reshape(b, l, SSD_HEADS, SSD_HEAD_DIM)
    bm = bm.reshape(b, l, SSD_GROUPS, SSD_STATE)
    cm = cm.reshape(b, l, SSD_GROUPS, SSD_STATE)
    dtf = jax.nn.softplus(dt_f + dt_bias_f)
    dtb = jax.nn.softplus(dt_b + dt_bias_b)
    a_f = -jnp.exp(a_log_f.astype(jnp.float32))
    a_b = -jnp.exp(a_log_b.astype(jnp.float32))
    xp, bp, cp = pad_front(xs), pad_front(bm), pad_front(cm)
    y_f = ssd_scan_forward(xp, pad_front(dtf), a_f, bp, cp)
    y_b = flip_seq(ssd_scan_forward(flip_seq(xp), flip_seq(pad_front(dtb)), a_b, flip_seq(bp), flip_seq(cp)))
    y = (y_f + y_b)[:, PAD:] + xs * d_skip[:, None]
    y = y.reshape(b, l, SSD_D_INNER) * jax.nn.silu(z)
    y_ssd = group_rms_norm(y, ssd_norm_w, SSD_GROUPS) @ w_ssd_out

    merged = jax.nn.sigmoid(gate_ret) * y_ret + jax.nn.sigmoid(gate_ssd) * y_ssd
    h = h + merged @ w_out

    f = dw_conv_centred(rms_norm(h, norm_ffn_w) @ w_ffn_up, w_ffn_conv, b_ffn_conv)
    fg, fu = _split(f, (D_FF, D_FF))
    return h + (jax.nn.silu(fg) * fu) @ w_ffn_down


def setup_inputs(seed: int = 0) -> dict:
    key = jax.random.key(seed)
    ks = jax.random.split(key, 24)
    f32 = jnp.float32

    def nrm(k, shape, scale):
        return jax.random.normal(k, shape, f32) * scale

    dt0 = jnp.exp(jax.random.uniform(ks[8], (2, DEPTH, SSD_HEADS), f32, minval=np.log(1e-3), maxval=np.log(1e-1)))
    dt_bias = dt0 + jnp.log(-jnp.expm1(-dt0))
    a_log = jnp.log(jax.random.uniform(ks[9], (2, DEPTH, SSD_HEADS), f32, minval=1.0, maxval=16.0))
    return {
        'x': nrm(ks[0], (BATCH, SEQ, D_MODEL), 1.0),
        'meta_tokens': nrm(ks[1], (N_META, D_MODEL), 1.0),
        'norm_mix_w': 1.0 + nrm(ks[2], (DEPTH, D_MODEL), 0.02),
        'w_in': nrm(ks[3], (DEPTH, D_MODEL, D_IN), D_MODEL ** -0.5),
        'ret_gn_w': 1.0 + nrm(ks[4], (DEPTH, RET_V), 0.02),
        'w_ret_out': nrm(ks[5], (DEPTH, RET_V, D_MODEL), RET_V ** -0.5),
        'w_ssd_conv': nrm(ks[6], (DEPTH, SSD_CONV, SSD_XBC), SSD_CONV ** -0.5),
        'b_ssd_conv': nrm(ks[7], (DEPTH, SSD_XBC), 0.02),
        'dt_bias_f': dt_bias[0],
        'dt_bias_b': dt_bias[1],
        'a_log_f': a_log[0],
        'a_log_b': a_log[1],
        'd_skip': 1.0 + nrm(ks[10], (DEPTH, SSD_HEADS), 0.02),
        'ssd_norm_w': 1.0 + nrm(ks[11], (DEPTH, SSD_D_INNER), 0.02),
        'w_ssd_out': nrm(ks[12], (DEPTH, SSD_D_INNER, D_MODEL), SSD_D_INNER ** -0.5),
        'w_out': nrm(ks[13], (DEPTH, D_MODEL, D_MODEL), D_MODEL ** -0.5),
        'norm_ffn_w': 1.0 + nrm(ks[14], (DEPTH, D_MODEL), 0.02),
        'w_ffn_up': nrm(ks[15], (DEPTH, D_MODEL, 2 * D_FF), D_MODEL ** -0.5),
        'w_ffn_conv': nrm(ks[16], (DEPTH, FFN_CONV, 2 * D_FF), FFN_CONV ** -0.5),
        'b_ffn_conv': nrm(ks[17], (DEPTH, 2 * D_FF), 0.02),
        'w_ffn_down': nrm(ks[18], (DEPTH, D_FF, D_MODEL), D_FF ** -0.5),
        'final_norm_w': 1.0 + nrm(ks[19], (D_MODEL,), 0.02),
    }


def reference(x, meta_tokens, norm_mix_w, w_in, ret_gn_w, w_ret_out, w_ssd_conv, b_ssd_conv,
              dt_bias_f, dt_bias_b, a_log_f, a_log_b, d_skip, ssd_norm_w, w_ssd_out, w_out,
              norm_ffn_w, w_ffn_up, w_ffn_conv, b_ffn_conv, w_ffn_down, final_norm_w):
    b = x.shape[0]
    meta = jnp.broadcast_to(meta_tokens[None].astype(x.dtype), (b, N_META, D_MODEL))
    h = jnp.concatenate([meta, x], axis=1)
    pos = jnp.arange(h.shape[1])
    for i in range(DEPTH):
        h = hybrid_layer(h, pos, norm_mix_w[i], w_in[i], ret_gn_w[i], w_ret_out[i], w_ssd_conv[i], b_ssd_conv[i],
                         dt_bias_f[i], dt_bias_b[i], a_log_f[i], a_log_b[i], d_skip[i], ssd_norm_w[i], w_ssd_out[i],
                         w_out[i], norm_ffn_w[i], w_ffn_up[i], w_ffn_conv[i], b_ffn_conv[i], w_ffn_down[i])
    h = rms_norm(h, final_norm_w)
    return h[:, N_META:]
```

```python
import functools
import math

import jax
import jax.numpy as jnp
import numpy as np
from jax import lax
from jax.experimental import pallas as pl
from jax.experimental.pallas import tpu as pltpu

F32 = jnp.float32
BF16 = jnp.bfloat16

D_MODEL = 1024
N_META = 16
CHUNK = 128
PAD = CHUNK - N_META
RET_HEADS = 4
RET_QK_DIM = 128
RET_V_DIM = 256
RET_QK = RET_HEADS * RET_QK_DIM
RET_V = RET_HEADS * RET_V_DIM
SSD_D_INNER = 2 * D_MODEL
SSD_HEAD_DIM = 64
SSD_HEADS = SSD_D_INNER // SSD_HEAD_DIM
SSD_GROUPS = 4
SSD_HPG = SSD_HEADS // SSD_GROUPS
SSD_STATE = 128
SSD_GW = SSD_HPG * SSD_HEAD_DIM
SSD_BC = SSD_GROUPS * SSD_STATE
D_FF = 2816
EPS = 1e-6
ROPE_BASE = 10000.0
LOG_GAMMA = tuple(math.log(1.0 - 2.0 ** (-5.0 - h)) for h in range(RET_HEADS))

COL_Z = 0
COL_XS = 2048
COL_B = 4096
COL_C = 4608
COL_Q = 5120
COL_K = 5632
COL_V = 6144
COL_G = 7168
COL_GATES = 8192
PROJ_W = 10240
PROJ_TN = 1024
QK_TILE = COL_Q // PROJ_TN
DT_W = 128

HALO = 16
LANES = 128
VMEM_LIMIT = 56 * 1024 * 1024


def _silu(x):
    return x * jax.nn.sigmoid(x)


def _softplus(x):
    return jnp.maximum(x, 0.0) + jnp.log1p(jnp.exp(-jnp.abs(x)))


def _split3(x):
    hi = x.astype(BF16)
    r1 = x - hi.astype(F32)
    mid = r1.astype(BF16)
    lo = (r1 - mid.astype(F32)).astype(BF16)
    return hi, mid, lo


def _dot(a, b):
    return jnp.dot(a, b, preferred_element_type=F32)


def _dot_tn(a, b):
    return lax.dot_general(a, b, (((0,), (0,)), ((), ())), preferred_element_type=F32)


def _dot_nt(a, b):
    return lax.dot_general(a, b, (((1,), (1,)), ((), ())), preferred_element_type=F32)


def _row_tile(rows, cap):
    best = CHUNK
    t = CHUNK
    while t <= min(rows, cap):
        if rows % t == 0:
            best = t
        t += CHUNK
    return best


def _proj_kernel(x_ref, nw_ref, w_ref, wdt_ref, cos_ref, sin_ref, o_ref, odt_ref, u_ref):
    j = pl.program_id(1)

    @pl.when(j == 0)
    def _():
        x = x_ref[...]
        ms = jnp.mean(x * x, axis=-1, keepdims=True)
        u = (x * lax.rsqrt(ms + EPS) * nw_ref[...]).astype(BF16)
        u_ref[...] = u
        odt_ref[...] = _dot(u, wdt_ref[...])

    acc = _dot(u_ref[...], w_ref[...])

    @pl.when(j != QK_TILE)
    def _():
        o_ref[...] = acc.astype(BF16)

    @pl.when(j == QK_TILE)
    def _():
        cos = cos_ref[...]
        sin = sin_ref[...]
        for hb in range(2 * RET_HEADS):
            a = acc[:, hb * LANES:(hb + 1) * LANES]
            r = a * cos + pltpu.roll(a, RET_QK_DIM // 2, 1) * sin
            if hb >= RET_HEADS:
                r = r * (RET_QK_DIM ** -0.5)
            o_ref[:, hb * LANES:(hb + 1) * LANES] = r.astype(BF16)


def _proj_call(hm, nw, w, wdt, cos, sin, lm):
    rows = hm.shape[0]
    tm = _row_tile(lm, 1408)
    tpb = lm // tm
    return pl.pallas_call(
        _proj_kernel,
        grid=(rows // tm, PROJ_W // PROJ_TN),
        in_specs=[
            pl.BlockSpec((tm, D_MODEL), lambda i, j: (i, 0)),
            pl.BlockSpec((1, D_MODEL), lambda i, j: (0, 0)),
            pl.BlockSpec((D_MODEL, PROJ_TN), lambda i, j: (0, j)),
            pl.BlockSpec((D_MODEL, DT_W), lambda i, j: (0, 0)),
            pl.BlockSpec((tm, LANES), lambda i, j: (i % tpb, 0)),
            pl.BlockSpec((tm, LANES), lambda i, j: (i % tpb, 0)),
        ],
        out_specs=[
            pl.BlockSpec((tm, PROJ_TN), lambda i, j: (i, j)),
            pl.BlockSpec((tm, DT_W), lambda i, j: (i, 0)),
        ],
        out_shape=[
            jax.ShapeDtypeStruct((rows, PROJ_W), BF16),
            jax.ShapeDtypeStruct((rows, DT_W), F32),
        ],
        scratch_shapes=[pltpu.VMEM((tm, D_MODEL), BF16)],
        compiler_params=pltpu.CompilerParams(
            dimension_semantics=("parallel", "arbitrary"), vmem_limit_bytes=VMEM_LIMIT),
        name="in_proj",
    )(hm, nw, w, wdt, cos, sin)


def _chunk_pos():
    return lax.broadcasted_iota(jnp.int32, (CHUNK, 1), 0).astype(F32)


def _ret_bstate_kernel(k_ref, v_ref, o_ref, r_ref):
    @pl.when(pl.program_id(1) == 0)
    def _():
        r_ref[...] = jnp.zeros_like(r_ref)

    pos = _chunk_pos()
    for h in range(RET_HEADS):
        lg = LOG_GAMMA[h]
        r = r_ref[h]
        o_ref[0, 0, h * RET_QK_DIM:(h + 1) * RET_QK_DIM, :] = r.astype(BF16)
        k = k_ref[0, :, h * RET_QK_DIM:(h + 1) * RET_QK_DIM].astype(F32)
        kd = (k * jnp.exp(lg * pos)).astype(BF16)
        v = v_ref[0, :, h * RET_V_DIM:(h + 1) * RET_V_DIM]
        r_ref[h] = math.exp(CHUNK * lg) * r + _dot_tn(kd, v)


def _ret_main_kernel(q_ref, k_ref, v_ref, g_ref, rb_ref, gnw_ref, o_ref, r_ref):
    @pl.when(pl.program_id(1) == 0)
    def _():
        r_ref[...] = jnp.zeros_like(r_ref)

    pos = _chunk_pos()
    li = lax.broadcasted_iota(jnp.int32, (CHUNK, CHUNK), 0)
    si = lax.broadcasted_iota(jnp.int32, (CHUNK, CHUNK), 1)
    dist = jnp.abs(li - si).astype(F32)
    for h in range(RET_HEADS):
        lg = LOG_GAMMA[h]
        q = q_ref[0, :, h * RET_QK_DIM:(h + 1) * RET_QK_DIM]
        k = k_ref[0, :, h * RET_QK_DIM:(h + 1) * RET_QK_DIM]
        v = v_ref[0, :, h * RET_V_DIM:(h + 1) * RET_V_DIM]
        r = r_ref[h]
        s = _dot_nt(q, k) * jnp.exp(lg * dist)
        y = _dot(s.astype(BF16), v)
        y = y + jnp.exp(lg * (pos + 1.0)) * _dot(q, r.astype(BF16))
        rb = rb_ref[0, 0, h * RET_QK_DIM:(h + 1) * RET_QK_DIM, :]
        y = y + jnp.exp(lg * (CHUNK - pos)) * _dot(q, rb)
        kd = (k.astype(F32) * jnp.exp(lg * (CHUNK - 1.0 - pos))).astype(BF16)
        r_ref[h] = math.exp(CHUNK * lg) * r + _dot_tn(kd, v)
        mu = jnp.mean(y, axis=-1, keepdims=True)
        yc = y - mu
        var = jnp.mean(yc * yc, axis=-1, keepdims=True)
        yn = yc * lax.rsqrt(var + EPS)
        sl = slice(h * RET_V_DIM, (h + 1) * RET_V_DIM)
        g = g_ref[0, :, sl].astype(F32)
        o_ref[0, :, sl] = (_silu(g) * (yn * gnw_ref[:, sl])).astype(BF16)


def _mem_block(c, nr):
    return jnp.where(c == 0, nr, c - 1)


def _retention(proj3, gnw, nb, nc):
    nr = nc - 1
    qb = RET_QK
    vb = RET_V

    def rev(s):
        return _mem_block(nc - 1 - s, nr)

    rb = pl.pallas_call(
        _ret_bstate_kernel,
        grid=(nb, nc),
        in_specs=[
            pl.BlockSpec((1, CHUNK, qb), lambda b, s: (b, rev(s), COL_K // qb)),
            pl.BlockSpec((1, CHUNK, vb), lambda b, s: (b, rev(s), COL_V // vb)),
        ],
        out_specs=pl.BlockSpec((1, 1, RET_QK, RET_V_DIM), lambda b, s: (b, nc - 1 - s, 0, 0)),
        out_shape=jax.ShapeDtypeStruct((nb, nc, RET_QK, RET_V_DIM), BF16),
        scratch_shapes=[pltpu.VMEM((RET_HEADS, RET_QK_DIM, RET_V_DIM), F32)],
        compiler_params=pltpu.CompilerParams(
            dimension_semantics=("parallel", "arbitrary"), vmem_limit_bytes=VMEM_LIMIT),
        name="ret_bwd_state",
    )(proj3, proj3)

    def mb(c):
        return _mem_block(c, nr)

    return pl.pallas_call(
        _ret_main_kernel,
        grid=(nb, nc),
        in_specs=[
            pl.BlockSpec((1, CHUNK, qb), lambda b, c: (b, mb(c), COL_Q // qb)),
            pl.BlockSpec((1, CHUNK, qb), lambda b, c: (b, mb(c), COL_K // qb)),
            pl.BlockSpec((1, CHUNK, vb), lambda b, c: (b, mb(c), COL_V // vb)),
            pl.BlockSpec((1, CHUNK, vb), lambda b, c: (b, mb(c), COL_G // vb)),
            pl.BlockSpec((1, 1, RET_QK, RET_V_DIM), lambda b, c: (b, c, 0, 0)),
            pl.BlockSpec((1, RET_V), lambda b, c: (0, 0)),
        ],
        out_specs=pl.BlockSpec((1, CHUNK, RET_V), lambda b, c: (b, mb(c), 0)),
        out_shape=jax.ShapeDtypeStruct((nb, nc * CHUNK, RET_V), BF16),
        scratch_shapes=[pltpu.VMEM((RET_HEADS, RET_QK_DIM, RET_V_DIM), F32)],
        compiler_params=pltpu.CompilerParams(
            dimension_semantics=("parallel", "arbitrary"), vmem_limit_bytes=VMEM_LIMIT),
        name="ret_main",
    )(proj3, proj3, proj3, proj3, rb, gnw)


def _conv_silu(main_ref, prev_ref, next_ref, w_ref, b_ref, buf_ref, valid):
    buf_ref[0:8, :] = prev_ref[0, HALO - 8:HALO, :].astype(F32)
    buf_ref[8:8 + CHUNK, :] = main_ref[0].astype(F32)
    buf_ref[8 + CHUNK:16 + CHUNK, :] = next_ref[0, 0:8, :].astype(F32)
    pre = (w_ref[0:1, :] * buf_ref[7:7 + CHUNK, :] + w_ref[1:2, :] * buf_ref[8:8 + CHUNK, :]
           + w_ref[2:3, :] * buf_ref[9:9 + CHUNK, :] + b_ref[...])
    return _silu(pre) * valid


def _valid_rows(is_meta_chunk):
    row = lax.broadcasted_iota(jnp.int32, (CHUNK, 1), 0)
    return jnp.where(jnp.logical_and(is_meta_chunk, row < PAD), 0.0, 1.0).astype(F32)


def _ssd_decays(dt_ref, dtb_ref, alog_ref, valid):
    dt = _softplus(dt_ref[0] + dtb_ref[...]) * valid
    da = dt * (-jnp.exp(alog_ref[...]))
    li = lax.broadcasted_iota(jnp.int32, (CHUNK, CHUNK), 0)
    si = lax.broadcasted_iota(jnp.int32, (CHUNK, CHUNK), 1)
    tri = jnp.where(si <= li, 1.0, 0.0).astype(BF16)
    hi, mid, lo = _split3(da)
    acs = _dot(tri, hi) + _dot(tri, mid) + _dot(tri, lo)
    return dt, da, acs


def _expand(x, e_ref):
    hi = x.astype(BF16)
    lo = (x - hi.astype(F32)).astype(BF16)
    return _dot(hi, e_ref[...]) + _dot(lo, e_ref[...])


def _ssd_bstate_kernel(xs_ref, xsp_ref, xsn_ref, b_ref, bp_ref, bn_ref, dt_ref,
                       cwx_ref, cbx_ref, cwb_ref, cbb_ref, dtb_ref, alog_ref, eb_ref,
                       o_ref, s_ref, bufx_ref, bufb_ref):
    step = pl.program_id(1)

    @pl.when(step == 0)
    def _():
        s_ref[...] = jnp.zeros_like(s_ref)

    valid = _valid_rows(step == pl.num_programs(1) - 1)
    xs = _conv_silu(xs_ref, xsp_ref, xsn_ref, cwx_ref, cbx_ref, bufx_ref, valid)
    bm = _conv_silu(b_ref, bp_ref, bn_ref, cwb_ref, cbb_ref, bufb_ref, valid).astype(BF16)
    dt, da, acs = _ssd_decays(dt_ref, dtb_ref, alog_ref, valid)
    ecs = acs - da
    w_exp = _expand(dt * jnp.exp(ecs), eb_ref)
    cdec = _expand(jnp.exp(acs), eb_ref)[CHUNK - 1:CHUNK, :]
    xw = (xs * w_exp).astype(BF16)
    for g in range(SSD_GROUPS):
        sl = slice(g * SSD_GW, (g + 1) * SSD_GW)
        s = s_ref[g]
        o_ref[0, 0, g * SSD_STATE:(g + 1) * SSD_STATE, :] = s.astype(BF16)
        s_ref[g] = cdec[:, sl] * s + _dot_tn(bm[:, g * SSD_STATE:(g + 1) * SSD_STATE], xw[:, sl])


def _ssd_main_kernel(z_ref, xs_ref, xsp_ref, xsn_ref, bc_ref, bcp_ref, bcn_ref, dt_ref, sb_ref,
                     cwx_ref, cbx_ref, cwbc_ref, cbbc_ref, dtb_ref, alog_ref, ef_ref, eb_ref,
                     dskip_ref, nw_ref, o_ref, s_ref, bufx_ref, bufbc_ref, y_ref):
    step = pl.program_id(1)

    @pl.when(step == 0)
    def _():
        s_ref[...] = jnp.zeros_like(s_ref)

    valid = _valid_rows(step == 0)
    xs = _conv_silu(xs_ref, xsp_ref, xsn_ref, cwx_ref, cbx_ref, bufx_ref, valid)
    bc = _conv_silu(bc_ref, bcp_ref, bcn_ref, cwbc_ref, cbbc_ref, bufbc_ref, valid).astype(BF16)
    dt, da, acs = _ssd_decays(dt_ref, dtb_ref, alog_ref, valid)
    ecs = acs - da
    last = acs[CHUNK - 1:CHUNK, :]
    acs_t = acs.T
    ecs_t = ecs.T
    dt_t = dt.T
    xs_b = xs.astype(BF16)

    li = lax.broadcasted_iota(jnp.int32, (CHUNK, CHUNK), 0)
    si = lax.broadcasted_iota(jnp.int32, (CHUNK, CHUNK), 1)
    lower = si <= li
    upper = si >= li
    lane = lax.broadcasted_iota(jnp.int32, (CHUNK, LANES), 1)
    first_half = lane < SSD_HEAD_DIM

    cb = []
    for g in range(SSD_GROUPS):
        bg = bc[:, g * SSD_STATE:(g + 1) * SSD_STATE]
        cg = bc[:, SSD_BC + g * SSD_STATE:SSD_BC + (g + 1) * SSD_STATE]
        cb.append(_dot_nt(cg, bg))

    def mix(h, fwd):
        g = h // SSD_HPG
        ln = h if fwd else SSD_HEADS + h
        if fwd:
            diff = acs[:, ln:ln + 1] - acs_t[ln:ln + 1, :]
            lmat = jnp.exp(jnp.where(lower, diff, -jnp.inf))
        else:
            diff = ecs_t[ln:ln + 1, :] - ecs[:, ln:ln + 1]
            lmat = jnp.exp(jnp.where(upper, diff, -jnp.inf))
        return (cb[g] * lmat * dt_t[ln:ln + 1, :]).astype(BF16)

    for p in range(SSD_HEADS // 2):
        xp = xs_b[:, p * LANES:(p + 1) * LANES]
        rhs = jnp.concatenate([jnp.where(first_half, xp, jnp.zeros_like(xp)),
                               jnp.where(first_half, jnp.zeros_like(xp), xp)], axis=0)
        mf = jnp.concatenate([mix(2 * p, True), mix(2 * p + 1, True)], axis=1)
        mb = jnp.concatenate([mix(2 * p, False), mix(2 * p + 1, False)], axis=1)
        y_ref[:, p * LANES:(p + 1) * LANES] = _dot(mf, rhs) + _dot(mb, rhs)

    ea_f = _expand(jnp.exp(acs), ef_ref)
    ea_b = _expand(jnp.exp(last - ecs), eb_ref)
    w_f = _expand(dt * jnp.exp(last - acs), ef_ref)
    cdec = ea_f[CHUNK - 1:CHUNK, :]
    xw = (xs * w_f).astype(BF16)
    for g in range(SSD_GROUPS):
        sl = slice(g * SSD_GW, (g + 1) * SSD_GW)
        bg = bc[:, g * SSD_STATE:(g + 1) * SSD_STATE]
        cg = bc[:, SSD_BC + g * SSD_STATE:SSD_BC + (g + 1) * SSD_STATE]
        s = s_ref[g]
        y_off = ea_f[:, sl] * _dot(cg, s.astype(BF16))
        y_off = y_off + ea_b[:, sl] * _dot(cg, sb_ref[0, 0, g * SSD_STATE:(g + 1) * SSD_STATE, :])
        s_ref[g] = cdec[:, sl] * s + _dot_tn(bg, xw[:, sl])
        y = y_ref[:, sl] + y_off + xs[:, sl] * dskip_ref[:, sl]
        y = y * _silu(z_ref[0, :, sl].astype(F32))
        ms = jnp.mean(y * y, axis=-1, keepdims=True)
        o_ref[0, :, sl] = (y * lax.rsqrt(ms + EPS) * nw_ref[:, sl]).astype(BF16)


def _ssd(proj3, dt3, cw, cbias, dtbias, alog, e_f, e_b, dskip, nw, nb, nc):
    nr = nc - 1
    nh = nc * CHUNK // HALO
    hpc = CHUNK // HALO

    def prev_blk(m):
        return lax.rem(m * hpc - 1 + nh, nh)

    def next_blk(m):
        return lax.rem((m + 1) * hpc, nh)

    cwx, cwb, cwbc = cw[:, :SSD_D_INNER], cw[:, SSD_D_INNER:SSD_D_INNER + SSD_BC], cw[:, SSD_D_INNER:]
    cbx, cbb, cbbc = (cbias[:, :SSD_D_INNER], cbias[:, SSD_D_INNER:SSD_D_INNER + SSD_BC],
                      cbias[:, SSD_D_INNER:])
    xw_, bw_, bcw_ = SSD_D_INNER, SSD_BC, 2 * SSD_BC

    def const(shape):
        return pl.BlockSpec(shape, lambda b, s: (0,) * len(shape))

    def rev(s):
        return _mem_block(nc - 1 - s, nr)

    sb = pl.pallas_call(
        _ssd_bstate_kernel,
        grid=(nb, nc),
        in_specs=[
            pl.BlockSpec((1, CHUNK, xw_), lambda b, s: (b, rev(s), COL_XS // xw_)),
            pl.BlockSpec((1, HALO, xw_), lambda b, s: (b, prev_blk(rev(s)), COL_XS // xw_)),
            pl.BlockSpec((1, HALO, xw_), lambda b, s: (b, next_blk(rev(s)), COL_XS // xw_)),
            pl.BlockSpec((1, CHUNK, bw_), lambda b, s: (b, rev(s), COL_B // bw_)),
            pl.BlockSpec((1, HALO, bw_), lambda b, s: (b, prev_blk(rev(s)), COL_B // bw_)),
            pl.BlockSpec((1, HALO, bw_), lambda b, s: (b, next_blk(rev(s)), COL_B // bw_)),
            pl.BlockSpec((1, CHUNK, DT_W), lambda b, s: (b, rev(s), 0)),
            const((3, xw_)), const((1, xw_)), const((3, bw_)), const((1, bw_)),
            const((1, DT_W)), const((1, DT_W)), const((LANES, SSD_D_INNER)),
        ],
        out_specs=pl.BlockSpec((1, 1, SSD_BC, SSD_GW), lambda b, s: (b, nc - 1 - s, 0, 0)),
        out_shape=jax.ShapeDtypeStruct((nb, nc, SSD_BC, SSD_GW), BF16),
        scratch_shapes=[
            pltpu.VMEM((SSD_GROUPS, SSD_STATE, SSD_GW), F32),
            pltpu.VMEM((CHUNK + 16, xw_), F32),
            pltpu.VMEM((CHUNK + 16, bw_), F32),
        ],
        compiler_params=pltpu.CompilerParams(
            dimension_semantics=("parallel", "arbitrary"), vmem_limit_bytes=VMEM_LIMIT),
        name="ssd_bwd_state",
    )(proj3, proj3, proj3, proj3, proj3, proj3, dt3, cwx, cbx, cwb, cbb, dtbias, alog, e_b)

    def mb(c):
        return _mem_block(c, nr)

    return pl.pallas_call(
        _ssd_main_kernel,
        grid=(nb, nc),
        in_specs=[
            pl.BlockSpec((1, CHUNK, xw_), lambda b, c: (b, mb(c), COL_Z // xw_)),
            pl.BlockSpec((1, CHUNK, xw_), lambda b, c: (b, mb(c), COL_XS // xw_)),
            pl.BlockSpec((1, HALO, xw_), lambda b, c: (b, prev_blk(mb(c)), COL_XS // xw_)),
            pl.BlockSpec((1, HALO, xw_), lambda b, c: (b, next_blk(mb(c)), COL_XS // xw_)),
            pl.BlockSpec((1, CHUNK, bcw_), lambda b, c: (b, mb(c), COL_B // bcw_)),
            pl.BlockSpec((1, HALO, bcw_), lambda b, c: (b, prev_blk(mb(c)), COL_B // bcw_)),
            pl.BlockSpec((1, HALO, bcw_), lambda b, c: (b, next_blk(mb(c)), COL_B // bcw_)),
            pl.BlockSpec((1, CHUNK, DT_W), lambda b, c: (b, mb(c), 0)),
            pl.BlockSpec((1, 1, SSD_BC, SSD_GW), lambda b, c: (b, c, 0, 0)),
            const((3, xw_)), const((1, xw_)), const((3, bcw_)), const((1, bcw_)),
            const((1, DT_W)), const((1, DT_W)),
            const((LANES, SSD_D_INNER)), const((LANES, SSD_D_INNER)),
            const((1, SSD_D_INNER)), const((1, SSD_D_INNER)),
        ],
        out_specs=pl.BlockSpec((1, CHUNK, SSD_D_INNER), lambda b, c: (b, mb(c), 0)),
        out_shape=jax.ShapeDtypeStruct((nb, nc * CHUNK, SSD_D_INNER), BF16),
        scratch_shapes=[
            pltpu.VMEM((SSD_GROUPS, SSD_STATE, SSD_GW), F32),
            pltpu.VMEM((CHUNK + 16, xw_), F32),
            pltpu.VMEM((CHUNK + 16, bcw_), F32),
            pltpu.VMEM((CHUNK, SSD_D_INNER), F32),
        ],
        compiler_params=pltpu.CompilerParams(
            dimension_semantics=("parallel", "arbitrary"), vmem_limit_bytes=VMEM_LIMIT),
        name="ssd_main",
    )(proj3, proj3, proj3, proj3, proj3, proj3, proj3, dt3, sb,
      cwx, cbx, cwbc, cbbc, dtbias, alog, e_f, e_b, dskip, nw)


def _merge_kernel(h_ref, yr_ref, ys_ref, gates_ref, wr_ref, ws_ref, wo_ref, o_ref):
    yr = _dot(yr_ref[...], wr_ref[...])
    ys = _dot(ys_ref[...], ws_ref[...])
    gr = jax.nn.sigmoid(gates_ref[:, :D_MODEL].astype(F32))
    gs = jax.nn.sigmoid(gates_ref[:, D_MODEL:].astype(F32))
    merged = (gr * yr + gs * ys).astype(BF16)
    o_ref[...] = h_ref[...] + _dot(merged, wo_ref[...])


def _merge_call(hm, yr, ys, proj, wr, ws, wo, lm):
    rows = hm.shape[0]
    tm = _row_tile(lm, 704)
    gw = 2 * D_MODEL
    return pl.pallas_call(
        _merge_kernel,
        grid=(rows // tm,),
        in_specs=[
            pl.BlockSpec((tm, D_MODEL), lambda i: (i, 0)),
            pl.BlockSpec((tm, RET_V), lambda i: (i, 0)),
            pl.BlockSpec((tm, SSD_D_INNER), lambda i: (i, 0)),
            pl.BlockSpec((tm, gw), lambda i: (i, COL_GATES // gw)),
            pl.BlockSpec((RET_V, D_MODEL), lambda i: (0, 0)),
            pl.BlockSpec((SSD_D_INNER, D_MODEL), lambda i: (0, 0)),
            pl.BlockSpec((D_MODEL, D_MODEL), lambda i: (0, 0)),
        ],
        out_specs=pl.BlockSpec((tm, D_MODEL), lambda i: (i, 0)),
        out_shape=jax.ShapeDtypeStruct((rows, D_MODEL), F32),
        compiler_params=pltpu.CompilerParams(
            dimension_semantics=("parallel",), vmem_limit_bytes=VMEM_LIMIT),
        name="merge_out",
    )(hm, yr, ys, proj, wr, ws, wo)


def _ffn_kernel(h_ref, hp_ref, hn_ref, nw_ref, wg_ref, wu_ref, cwg_ref, cwu_ref, cbg_ref, cbu_ref,
                wd_ref, fnw_ref, o_ref, u_ref, fg_ref, fu_ref, acc_ref, *, tm):
    j = pl.program_id(2)

    def norm(x):
        ms = jnp.mean(x * x, axis=-1, keepdims=True)
        return (x * lax.rsqrt(ms + EPS) * nw_ref[...]).astype(BF16)

    @pl.when(j == 0)
    def _():
        u_ref[0:HALO, :] = norm(hp_ref[0])
        u_ref[HALO:HALO + tm, :] = norm(h_ref[0])
        u_ref[HALO + tm:2 * HALO + tm, :] = norm(hn_ref[0])
        acc_ref[...] = jnp.zeros_like(acc_ref)

    u = u_ref[...]
    fg_ref[...] = _dot(u, wg_ref[...])
    fu_ref[...] = _dot(u, wu_ref[...])

    def conv(f_ref, cw_ref, cb_ref):
        return (cw_ref[0:1, :] * f_ref[HALO - 1:HALO - 1 + tm, :] + cw_ref[1:2, :] * f_ref[HALO:HALO + tm, :]
                + cw_ref[2:3, :] * f_ref[HALO + 1:HALO + 1 + tm, :] + cb_ref[...])

    act = (_silu(conv(fg_ref, cwg_ref, cbg_ref)) * conv(fu_ref, cwu_ref, cbu_ref)).astype(BF16)
    acc_ref[...] += _dot(act, wd_ref[...])

    @pl.when(j == pl.num_programs(2) - 1)
    def _():
        y = h_ref[0] + acc_ref[...]
        ms = jnp.mean(y * y, axis=-1, keepdims=True)
        o_ref[0] = y * lax.rsqrt(ms + EPS) * fnw_ref[...]


def _ffn_call(hmid3, nw, w_up, cw, cb, w_down, fnw, nb, s_len):
    lm = hmid3.shape[1]
    tm = 512 if s_len % 512 == 0 else CHUNK
    tn = 256
    nj = D_FF // tn
    nh = lm // HALO
    hpt = tm // HALO

    def prev_blk(t):
        return jnp.where(t == 0, nh - 1, t * hpt - 1)

    return pl.pallas_call(
        functools.partial(_ffn_kernel, tm=tm),
        grid=(nb, s_len // tm, nj),
        in_specs=[
            pl.BlockSpec((1, tm, D_MODEL), lambda b, t, j: (b, t, 0)),
            pl.BlockSpec((1, HALO, D_MODEL), lambda b, t, j: (b, prev_blk(t), 0)),
            pl.BlockSpec((1, HALO, D_MODEL), lambda b, t, j: (b, (t + 1) * hpt, 0)),
            pl.BlockSpec((1, D_MODEL), lambda b, t, j: (0, 0)),
            pl.BlockSpec((D_MODEL, tn), lambda b, t, j: (0, j)),
            pl.BlockSpec((D_MODEL, tn), lambda b, t, j: (0, nj + j)),
            pl.BlockSpec((3, tn), lambda b, t, j: (0, j)),
            pl.BlockSpec((3, tn), lambda b, t, j: (0, nj + j)),
            pl.BlockSpec((1, tn), lambda b, t, j: (0, j)),
            pl.BlockSpec((1, tn), lambda b, t, j: (0, nj + j)),
            pl.BlockSpec((tn, D_MODEL), lambda b, t, j: (j, 0)),
            pl.BlockSpec((1, D_MODEL), lambda b, t, j: (0, 0)),
        ],
        out_specs=pl.BlockSpec((1, tm, D_MODEL), lambda b, t, j: (b, t, 0)),
        out_shape=jax.ShapeDtypeStruct((nb, s_len, D_MODEL), F32),
        scratch_shapes=[
            pltpu.VMEM((tm + 2 * HALO, D_MODEL), BF16),
            pltpu.VMEM((tm + 2 * HALO, tn), F32),
            pltpu.VMEM((tm + 2 * HALO, tn), F32),
            pltpu.VMEM((tm, D_MODEL), F32),
        ],
        compiler_params=pltpu.CompilerParams(
            dimension_semantics=("parallel", "parallel", "arbitrary"), vmem_limit_bytes=VMEM_LIMIT),
        name="ffn",
    )(hmid3, hmid3, hmid3, nw, w_up, w_up, cw, cw, cb, cb, w_down, fnw)


def _rope_tables(s_len):
    half = RET_QK_DIM // 2
    inv = ROPE_BASE ** (-jnp.arange(half, dtype=F32) / half)
    pos = jnp.concatenate([N_META + jnp.arange(s_len), jnp.zeros((PAD,), jnp.int32), jnp.arange(N_META)])
    ang = pos.astype(F32)[:, None] * inv[None, :]
    cos, sin = jnp.cos(ang), jnp.sin(ang)
    return jnp.concatenate([cos, cos], axis=1), jnp.concatenate([-sin, sin], axis=1)


def _head_expander(offset):
    e = np.zeros((LANES, SSD_D_INNER), np.float32)
    for h in range(SSD_HEADS):
        e[offset + h, h * SSD_HEAD_DIM:(h + 1) * SSD_HEAD_DIM] = 1.0
    return jnp.asarray(e, BF16)


def kernel(x, meta_tokens, norm_mix_w, w_in, ret_gn_w, w_ret_out, w_ssd_conv, b_ssd_conv, dt_bias_f, dt_bias_b, a_log_f, a_log_b, d_skip, ssd_norm_w, w_ssd_out, w_out, norm_ffn_w, w_ffn_up, w_ffn_conv, b_ffn_conv, w_ffn_down, final_norm_w):
    assert norm_mix_w.shape[0] == 1, "single-layer block"
    nb, s_len, _ = x.shape
    assert s_len % CHUNK == 0
    lm = s_len + CHUNK
    nc = lm // CHUNK

    tail = jnp.concatenate([jnp.zeros((PAD, D_MODEL), x.dtype), meta_tokens.astype(x.dtype)], axis=0)
    hm = jnp.concatenate([x, jnp.broadcast_to(tail[None], (nb, CHUNK, D_MODEL))], axis=1)
    hm2 = hm.reshape(nb * lm, D_MODEL)

    wi = w_in[0]
    offs = np.cumsum([0, RET_QK, RET_QK, RET_V, RET_V, SSD_D_INNER, SSD_D_INNER + 2 * SSD_BC,
                      SSD_HEADS, SSD_HEADS, D_MODEL, D_MODEL])
    seg = [wi[:, offs[i]:offs[i + 1]] for i in range(10)]
    w_main = jnp.concatenate([seg[4], seg[5], seg[0], seg[1], seg[2], seg[3], seg[8], seg[9]], axis=1).astype(BF16)
    w_dt = jnp.concatenate([seg[6], seg[7], jnp.zeros((D_MODEL, DT_W - 2 * SSD_HEADS), wi.dtype)], axis=1).astype(BF16)
    cos, sin = _rope_tables(s_len)

    proj, dtp = _proj_call(hm2, norm_mix_w, w_main, w_dt, cos, sin, lm)
    proj3 = proj.reshape(nb, lm, PROJ_W)
    dt3 = dtp.reshape(nb, lm, DT_W)

    y_ret = _retention(proj3, ret_gn_w, nb, nc)

    lane_pad = jnp.zeros((1, DT_W - 2 * SSD_HEADS), F32)
    dtbias = jnp.concatenate([dt_bias_f, dt_bias_b, lane_pad], axis=1)
    alog = jnp.concatenate([a_log_f, a_log_b, lane_pad], axis=1)
    dskip = jnp.repeat(d_skip, SSD_HEAD_DIM, axis=1)
    y_ssd = _ssd(proj3, dt3, w_ssd_conv[0], b_ssd_conv, dtbias, alog,
                 _head_expander(0), _head_expander(SSD_HEADS), dskip, ssd_norm_w, nb, nc)

    h_mid = _merge_call(hm2, y_ret.reshape(nb * lm, RET_V), y_ssd.reshape(nb * lm, SSD_D_INNER), proj,
                        w_ret_out[0].astype(BF16), w_ssd_out[0].astype(BF16), w_out[0].astype(BF16), lm)

    return _ffn_call(h_mid.reshape(nb, lm, D_MODEL), norm_ffn_w, w_ffn_up[0].astype(BF16), w_ffn_conv[0],
                     b_ffn_conv, w_ffn_down[0].astype(BF16), final_norm_w.reshape(1, D_MODEL), nb, s_len)
```

```python
import functools
import math

import jax
import jax.numpy as jnp
import numpy as np
from jax import lax
from jax.experimental import pallas as pl
from jax.experimental.pallas import tpu as pltpu

F32 = jnp.float32
BF16 = jnp.bfloat16

D_MODEL = 1024
N_META = 16
CHUNK = 128
PAD = CHUNK - N_META
RET_HEADS = 4
RET_QK_DIM = 128
RET_V_DIM = 256
RET_QK = RET_HEADS * RET_QK_DIM
RET_V = RET_HEADS * RET_V_DIM
SSD_D_INNER = 2 * D_MODEL
SSD_HEAD_DIM = 64
SSD_HEADS = SSD_D_INNER // SSD_HEAD_DIM
SSD_GROUPS = 4
SSD_HPG = SSD_HEADS // SSD_GROUPS
SSD_STATE = 128
SSD_GW = SSD_HPG * SSD_HEAD_DIM
SSD_BC = SSD_GROUPS * SSD_STATE
SSD_XBC = SSD_D_INNER + 2 * SSD_BC
D_FF = 2816
EPS = 1e-6
ROPE_BASE = 10000.0
LOG_GAMMA = tuple(math.log(1.0 - 2.0 ** (-5.0 - h)) for h in range(RET_HEADS))
LOG2E = math.log2(math.e)

COL_Z = 0
COL_XS = 2048
COL_B = 4096
COL_C = 4608
COL_Q = 5120
COL_K = 5632
COL_V = 6144
COL_G = 7168
COL_GATES = 8192
PROJ_W = 10240
PROJ_TN = 1024
QK_TILE = COL_Q // PROJ_TN
CONV_TILE0 = COL_XS // PROJ_TN
CONV_TILES = SSD_XBC // PROJ_TN
DT_W = 128

HALO = 16
LANES = 128
VMEM_LIMIT = 56 * 1024 * 1024


def _sigmoid(x):
    return 0.5 + 0.5 * jnp.tanh(0.5 * x)


def _silu(x):
    hx = 0.5 * x
    return hx + hx * jnp.tanh(hx)


def _softplus(x):
    return jnp.maximum(x, 0.0) + jnp.log1p(jnp.exp(-jnp.abs(x)))


def _split3(x):
    hi = x.astype(BF16)
    r1 = x - hi.astype(F32)
    mid = r1.astype(BF16)
    lo = (r1 - mid.astype(F32)).astype(BF16)
    return hi, mid, lo


def _dot(a, b):
    return jnp.dot(a, b, preferred_element_type=F32)


def _dot_tn(a, b):
    return lax.dot_general(a, b, (((0,), (0,)), ((), ())), preferred_element_type=F32)


def _dot_nt(a, b):
    return lax.dot_general(a, b, (((1,), (1,)), ((), ())), preferred_element_type=F32)


def _row_tile(rows, cap):
    best = CHUNK
    t = CHUNK
    while t <= min(rows, cap):
        if rows % t == 0:
            best = t
        t += CHUNK
    return best


def _rms(x, w):
    ms = jnp.mean(x * x, axis=-1, keepdims=True)
    return x * lax.rsqrt(ms + EPS) * w


def _proj_kernel(x_ref, xp_ref, xn_ref, nw_ref, w_ref, wdt_ref, cos_ref, sin_ref, valid_ref, cw_ref, cb_ref,
                 o_ref, odt_ref, u_ref, acc_ref, *, tm):
    j = pl.program_id(2)

    @pl.when(j == 0)
    def _():
        nw = nw_ref[...]
        u_ref[0:HALO, :] = _rms(xp_ref[0], nw).astype(BF16)
        u_ref[HALO:HALO + tm, :] = _rms(x_ref[0], nw).astype(BF16)
        u_ref[HALO + tm:2 * HALO + tm, :] = _rms(xn_ref[0], nw).astype(BF16)
        odt_ref[0] = _dot(u_ref[HALO:HALO + tm, :], wdt_ref[...])

    is_conv = jnp.logical_and(j >= CONV_TILE0, j < CONV_TILE0 + CONV_TILES)
    is_qk = j == QK_TILE

    @pl.when(is_conv)
    def _():
        acc_ref[...] = _dot(u_ref[...], w_ref[...])
        valid = valid_ref[...]
        for cbk in range(PROJ_TN // LANES):
            sl = slice(cbk * LANES, (cbk + 1) * LANES)
            pre = (cw_ref[0:1, sl] * acc_ref[HALO - 1:HALO - 1 + tm, sl] + cw_ref[1:2, sl] * acc_ref[HALO:HALO + tm, sl]
                   + cw_ref[2:3, sl] * acc_ref[HALO + 1:HALO + 1 + tm, sl] + cb_ref[:, sl])
            o_ref[0, :, sl] = (_silu(pre) * valid).astype(BF16)

    @pl.when(is_qk)
    def _():
        acc = _dot(u_ref[HALO:HALO + tm, :], w_ref[...])
        cos = cos_ref[...]
        sin = sin_ref[...]
        for hb in range(2 * RET_HEADS):
            a = acc[:, hb * LANES:(hb + 1) * LANES]
            r = a * cos + pltpu.roll(a, RET_QK_DIM // 2, 1) * sin
            if hb >= RET_HEADS:
                r = r * (RET_QK_DIM ** -0.5)
            o_ref[0, :, hb * LANES:(hb + 1) * LANES] = r.astype(BF16)

    @pl.when(jnp.logical_not(jnp.logical_or(is_conv, is_qk)))
    def _():
        o_ref[0] = _dot(u_ref[HALO:HALO + tm, :], w_ref[...]).astype(BF16)


def _proj_call(hm3, nw, w, wdt, cos, sin, valid, cw, cb):
    nb, lm, _ = hm3.shape
    tm = _row_tile(lm, 1408)
    tpb = lm // tm
    nh = lm // HALO
    hpt = tm // HALO

    def conv_blk(j):
        return jnp.clip(j - CONV_TILE0, 0, CONV_TILES - 1)

    return pl.pallas_call(
        functools.partial(_proj_kernel, tm=tm),
        grid=(nb, tpb, PROJ_W // PROJ_TN),
        in_specs=[
            pl.BlockSpec((1, tm, D_MODEL), lambda b, t, j: (b, t, 0)),
            pl.BlockSpec((1, HALO, D_MODEL), lambda b, t, j: (b, lax.rem(t * hpt - 1 + nh, nh), 0)),
            pl.BlockSpec((1, HALO, D_MODEL), lambda b, t, j: (b, lax.rem((t + 1) * hpt, nh), 0)),
            pl.BlockSpec((1, D_MODEL), lambda b, t, j: (0, 0)),
            pl.BlockSpec((D_MODEL, PROJ_TN), lambda b, t, j: (0, j)),
            pl.BlockSpec((D_MODEL, DT_W), lambda b, t, j: (0, 0)),
            pl.BlockSpec((tm, LANES), lambda b, t, j: (t, 0)),
            pl.BlockSpec((tm, LANES), lambda b, t, j: (t, 0)),
            pl.BlockSpec((tm, LANES), lambda b, t, j: (t, 0)),
            pl.BlockSpec((3, PROJ_TN), lambda b, t, j: (0, conv_blk(j))),
            pl.BlockSpec((1, PROJ_TN), lambda b, t, j: (0, conv_blk(j))),
        ],
        out_specs=[
            pl.BlockSpec((1, tm, PROJ_TN), lambda b, t, j: (b, t, j)),
            pl.BlockSpec((1, tm, DT_W), lambda b, t, j: (b, t, 0)),
        ],
        out_shape=[
            jax.ShapeDtypeStruct((nb, lm, PROJ_W), BF16),
            jax.ShapeDtypeStruct((nb, lm, DT_W), F32),
        ],
        scratch_shapes=[
            pltpu.VMEM((tm + 2 * HALO, D_MODEL), BF16),
            pltpu.VMEM((tm + 2 * HALO, PROJ_TN), F32),
        ],
        compiler_params=pltpu.CompilerParams(
            dimension_semantics=("parallel", "parallel", "arbitrary"), vmem_limit_bytes=VMEM_LIMIT),
        name="in_proj",
    )(hm3, hm3, hm3, nw, w, wdt, cos, sin, valid, cw, cb)


def _chunk_pos():
    return lax.broadcasted_iota(jnp.int32, (CHUNK, 1), 0).astype(F32)


def _ret_bstate_kernel(k_ref, v_ref, o_ref, r_ref):
    @pl.when(pl.program_id(1) == 0)
    def _():
        r_ref[...] = jnp.zeros_like(r_ref)

    pos = _chunk_pos()
    for h in range(RET_HEADS):
        lg = LOG_GAMMA[h]
        r = r_ref[h]
        o_ref[0, 0, h * RET_QK_DIM:(h + 1) * RET_QK_DIM, :] = r.astype(BF16)
        k = k_ref[0, :, h * RET_QK_DIM:(h + 1) * RET_QK_DIM].astype(F32)
        kd = (k * jnp.exp(lg * pos)).astype(BF16)
        v = v_ref[0, :, h * RET_V_DIM:(h + 1) * RET_V_DIM]
        r_ref[h] = math.exp(CHUNK * lg) * r + _dot_tn(kd, v)


def _ret_main_kernel(q_ref, k_ref, v_ref, g_ref, rb_ref, gnw_ref, o_ref, r_ref):
    @pl.when(pl.program_id(1) == 0)
    def _():
        r_ref[...] = jnp.zeros_like(r_ref)

    pos = _chunk_pos()
    li = lax.broadcasted_iota(jnp.int32, (CHUNK, CHUNK), 0)
    si = lax.broadcasted_iota(jnp.int32, (CHUNK, CHUNK), 1)
    dist = jnp.abs(li - si).astype(F32)
    for h in range(RET_HEADS):
        lg = LOG_GAMMA[h]
        q = q_ref[0, :, h * RET_QK_DIM:(h + 1) * RET_QK_DIM]
        k = k_ref[0, :, h * RET_QK_DIM:(h + 1) * RET_QK_DIM]
        v = v_ref[0, :, h * RET_V_DIM:(h + 1) * RET_V_DIM]
        r = r_ref[h]
        s = _dot_nt(q, k) * jnp.exp(lg * dist)
        y = _dot(s.astype(BF16), v)
        y = y + jnp.exp(lg * (pos + 1.0)) * _dot(q, r.astype(BF16))
        rb = rb_ref[0, 0, h * RET_QK_DIM:(h + 1) * RET_QK_DIM, :]
        y = y + jnp.exp(lg * (CHUNK - pos)) * _dot(q, rb)
        kd = (k.astype(F32) * jnp.exp(lg * (CHUNK - 1.0 - pos))).astype(BF16)
        r_ref[h] = math.exp(CHUNK * lg) * r + _dot_tn(kd, v)
        mu = jnp.mean(y, axis=-1, keepdims=True)
        yc = y - mu
        var = jnp.mean(yc * yc, axis=-1, keepdims=True)
        yn = yc * lax.rsqrt(var + EPS)
        sl = slice(h * RET_V_DIM, (h + 1) * RET_V_DIM)
        g = g_ref[0, :, sl].astype(F32)
        o_ref[0, :, sl] = (_silu(g) * (yn * gnw_ref[:, sl])).astype(BF16)


def _mem_block(c, nr):
    return jnp.where(c == 0, nr, c - 1)


def _retention(proj3, gnw, nb, nc):
    nr = nc - 1
    qb = RET_QK
    vb = RET_V

    def rev(s):
        return _mem_block(nc - 1 - s, nr)

    rb = pl.pallas_call(
        _ret_bstate_kernel,
        grid=(nb, nc),
        in_specs=[
            pl.BlockSpec((1, CHUNK, qb), lambda b, s: (b, rev(s), COL_K // qb)),
            pl.BlockSpec((1, CHUNK, vb), lambda b, s: (b, rev(s), COL_V // vb)),
        ],
        out_specs=pl.BlockSpec((1, 1, RET_QK, RET_V_DIM), lambda b, s: (b, nc - 1 - s, 0, 0)),
        out_shape=jax.ShapeDtypeStruct((nb, nc, RET_QK, RET_V_DIM), BF16),
        scratch_shapes=[pltpu.VMEM((RET_HEADS, RET_QK_DIM, RET_V_DIM), F32)],
        compiler_params=pltpu.CompilerParams(
            dimension_semantics=("parallel", "arbitrary"), vmem_limit_bytes=VMEM_LIMIT),
        name="ret_bwd_state",
    )(proj3, proj3)

    def mb(c):
        return _mem_block(c, nr)

    return pl.pallas_call(
        _ret_main_kernel,
        grid=(nb, nc),
        in_specs=[
            pl.BlockSpec((1, CHUNK, qb), lambda b, c: (b, mb(c), COL_Q // qb)),
            pl.BlockSpec((1, CHUNK, qb), lambda b, c: (b, mb(c), COL_K // qb)),
            pl.BlockSpec((1, CHUNK, vb), lambda b, c: (b, mb(c), COL_V // vb)),
            pl.BlockSpec((1, CHUNK, vb), lambda b, c: (b, mb(c), COL_G // vb)),
            pl.BlockSpec((1, 1, RET_QK, RET_V_DIM), lambda b, c: (b, c, 0, 0)),
            pl.BlockSpec((1, RET_V), lambda b, c: (0, 0)),
        ],
        out_specs=pl.BlockSpec((1, CHUNK, RET_V), lambda b, c: (b, mb(c), 0)),
        out_shape=jax.ShapeDtypeStruct((nb, nc * CHUNK, RET_V), BF16),
        scratch_shapes=[pltpu.VMEM((RET_HEADS, RET_QK_DIM, RET_V_DIM), F32)],
        compiler_params=pltpu.CompilerParams(
            dimension_semantics=("parallel", "arbitrary"), vmem_limit_bytes=VMEM_LIMIT),
        name="ret_main",
    )(proj3, proj3, proj3, proj3, rb, gnw)


def _valid_rows(is_meta_chunk):
    row = lax.broadcasted_iota(jnp.int32, (CHUNK, 1), 0)
    return jnp.where(jnp.logical_and(is_meta_chunk, row < PAD), 0.0, 1.0).astype(F32)


def _ssd_decays(dt_ref, dtb_ref, alog_ref, valid):
    dt = _softplus(dt_ref[0] + dtb_ref[...]) * valid
    da = dt * (-LOG2E * jnp.exp(alog_ref[...]))
    li = lax.broadcasted_iota(jnp.int32, (CHUNK, CHUNK), 0)
    si = lax.broadcasted_iota(jnp.int32, (CHUNK, CHUNK), 1)
    tri = jnp.where(si <= li, 1.0, 0.0).astype(BF16)
    hi, mid, lo = _split3(da)
    acs = _dot(tri, hi) + _dot(tri, mid) + _dot(tri, lo)
    return dt, da, acs


def _pack_hi_lo(x):
    lane = lax.broadcasted_iota(jnp.int32, x.shape, 1)
    hi = x.astype(BF16).astype(F32)
    return jnp.where(lane < LANES // 2, hi, pltpu.roll(x - hi, LANES // 2, 1)).astype(BF16)


def _ssd_bstate_kernel(xs_ref, b_ref, dt_ref, dtb_ref, alog_ref, eb_ref, o_ref, s_ref):
    step = pl.program_id(1)

    @pl.when(step == 0)
    def _():
        s_ref[...] = jnp.zeros_like(s_ref)

    valid = _valid_rows(step == pl.num_programs(1) - 1)
    dt, da, acs = _ssd_decays(dt_ref, dtb_ref, alog_ref, valid)
    stack = jnp.concatenate([dt * jnp.exp2(acs - da), jnp.exp2(acs[CHUNK - HALO:CHUNK, :])], axis=0)
    ex = _dot(_pack_hi_lo(stack), eb_ref[...])
    xw = (xs_ref[0].astype(F32) * ex[0:CHUNK, :]).astype(BF16)
    cdec = ex[CHUNK + HALO - 1:CHUNK + HALO, :]
    bm = b_ref[0]
    for g in range(SSD_GROUPS):
        sl = slice(g * SSD_GW, (g + 1) * SSD_GW)
        s = s_ref[g]
        o_ref[0, 0, g * SSD_STATE:(g + 1) * SSD_STATE, :] = s.astype(BF16)
        s_ref[g] = cdec[:, sl] * s + _dot_tn(bm[:, g * SSD_STATE:(g + 1) * SSD_STATE], xw[:, sl])


def _ssd_main_kernel(z_ref, xs_ref, bc_ref, dt_ref, sb_ref, dtb_ref, alog_ref, ef_ref, eb_ref,
                     dskip_ref, nw_ref, o_ref, s_ref, y_ref):
    step = pl.program_id(1)

    @pl.when(step == 0)
    def _():
        s_ref[...] = jnp.zeros_like(s_ref)

    valid = _valid_rows(step == 0)
    dt, da, acs = _ssd_decays(dt_ref, dtb_ref, alog_ref, valid)
    ecs = acs - da
    last = acs[CHUNK - 1:CHUNK, :]
    ldt_t = jnp.log2(dt).T
    row_f = acs.T - ldt_t
    row_b = ecs.T + ldt_t
    xs_b = xs_ref[0]
    bc = bc_ref[0]

    li = lax.broadcasted_iota(jnp.int32, (CHUNK, CHUNK), 0)
    si = lax.broadcasted_iota(jnp.int32, (CHUNK, CHUNK), 1)
    lower = si <= li
    upper = si >= li
    lane = lax.broadcasted_iota(jnp.int32, (CHUNK, LANES), 1)
    first_half = lane < SSD_HEAD_DIM
    zero = jnp.zeros((CHUNK, LANES), BF16)

    cb = []
    for g in range(SSD_GROUPS):
        bg = bc[:, g * SSD_STATE:(g + 1) * SSD_STATE]
        cg = bc[:, SSD_BC + g * SSD_STATE:SSD_BC + (g + 1) * SSD_STATE]
        cb.append(_dot_nt(cg, bg))

    def mix(h, fwd):
        g = h // SSD_HPG
        ln = h if fwd else SSD_HEADS + h
        if fwd:
            diff = jnp.where(lower, acs[:, ln:ln + 1] - row_f[ln:ln + 1, :], -jnp.inf)
        else:
            diff = jnp.where(upper, row_b[ln:ln + 1, :] - ecs[:, ln:ln + 1], -jnp.inf)
        return (cb[g] * jnp.exp2(diff)).astype(BF16)

    for p in range(SSD_HEADS // 2):
        xp = xs_b[:, p * LANES:(p + 1) * LANES]
        rhs = jnp.concatenate([jnp.where(first_half, xp, zero), jnp.where(first_half, zero, xp)], axis=0)
        mf = jnp.concatenate([mix(2 * p, True), mix(2 * p + 1, True)], axis=1)
        mb = jnp.concatenate([mix(2 * p, False), mix(2 * p + 1, False)], axis=1)
        y_ref[:, p * LANES:(p + 1) * LANES] = _dot(mf, rhs) + _dot(mb, rhs)

    ea = jnp.exp2(acs)
    exf = _dot(jnp.concatenate([_pack_hi_lo(ea), _pack_hi_lo(dt * jnp.exp2(last - acs))], axis=0), ef_ref[...])
    ea_f = exf[0:CHUNK, :]
    w_f = exf[CHUNK:2 * CHUNK, :]
    ea_b = _dot(_pack_hi_lo(jnp.exp2(last - ecs)), eb_ref[...])
    cdec = ea_f[CHUNK - 1:CHUNK, :]
    xs = xs_b.astype(F32)
    xw = (xs * w_f).astype(BF16)
    for g in range(SSD_GROUPS):
        sl = slice(g * SSD_GW, (g + 1) * SSD_GW)
        bg = bc[:, g * SSD_STATE:(g + 1) * SSD_STATE]
        cg = bc[:, SSD_BC + g * SSD_STATE:SSD_BC + (g + 1) * SSD_STATE]
        s = s_ref[g]
        y_off = ea_f[:, sl] * _dot(cg, s.astype(BF16))
        y_off = y_off + ea_b[:, sl] * _dot(cg, sb_ref[0, 0, g * SSD_STATE:(g + 1) * SSD_STATE, :])
        s_ref[g] = cdec[:, sl] * s + _dot_tn(bg, xw[:, sl])
        y = y_ref[:, sl] + y_off + xs[:, sl] * dskip_ref[:, sl]
        y = y * _silu(z_ref[0, :, sl].astype(F32))
        o_ref[0, :, sl] = _rms(y, nw_ref[:, sl]).astype(BF16)


def _head_expander(offset):
    e = np.zeros((LANES, SSD_D_INNER), np.float32)
    for h in range(SSD_HEADS):
        e[offset + h, h * SSD_HEAD_DIM:(h + 1) * SSD_HEAD_DIM] = 1.0
        e[LANES // 2 + offset + h, h * SSD_HEAD_DIM:(h + 1) * SSD_HEAD_DIM] = 1.0
    return jnp.asarray(e, BF16)


def _ssd(proj3, dt3, dtbias, alog, dskip, nw, nb, nc):
    nr = nc - 1
    xw_, bw_, bcw_ = SSD_D_INNER, SSD_BC, 2 * SSD_BC
    e_f = _head_expander(0)
    e_b = _head_expander(SSD_HEADS)

    def const(shape):
        return pl.BlockSpec(shape, lambda b, s: (0,) * len(shape))

    def rev(s):
        return _mem_block(nc - 1 - s, nr)

    sb = pl.pallas_call(
        _ssd_bstate_kernel,
        grid=(nb, nc),
        in_specs=[
            pl.BlockSpec((1, CHUNK, xw_), lambda b, s: (b, rev(s), COL_XS // xw_)),
            pl.BlockSpec((1, CHUNK, bw_), lambda b, s: (b, rev(s), COL_B // bw_)),
            pl.BlockSpec((1, CHUNK, DT_W), lambda b, s: (b, rev(s), 0)),
            const((1, DT_W)), const((1, DT_W)), const((LANES, SSD_D_INNER)),
        ],
        out_specs=pl.BlockSpec((1, 1, SSD_BC, SSD_GW), lambda b, s: (b, nc - 1 - s, 0, 0)),
        out_shape=jax.ShapeDtypeStruct((nb, nc, SSD_BC, SSD_GW), BF16),
        scratch_shapes=[pltpu.VMEM((SSD_GROUPS, SSD_STATE, SSD_GW), F32)],
        compiler_params=pltpu.CompilerParams(
            dimension_semantics=("parallel", "arbitrary"), vmem_limit_bytes=VMEM_LIMIT),
        name="ssd_bwd_state",
    )(proj3, proj3, dt3, dtbias, alog, e_b)

    def mb(c):
        return _mem_block(c, nr)

    return pl.pallas_call(
        _ssd_main_kernel,
        grid=(nb, nc),
        in_specs=[
            pl.BlockSpec((1, CHUNK, xw_), lambda b, c: (b, mb(c), COL_Z // xw_)),
            pl.BlockSpec((1, CHUNK, xw_), lambda b, c: (b, mb(c), COL_XS // xw_)),
            pl.BlockSpec((1, CHUNK, bcw_), lambda b, c: (b, mb(c), COL_B // bcw_)),
            pl.BlockSpec((1, CHUNK, DT_W), lambda b, c: (b, mb(c), 0)),
            pl.BlockSpec((1, 1, SSD_BC, SSD_GW), lambda b, c: (b, c, 0, 0)),
            const((1, DT_W)), const((1, DT_W)),
            const((LANES, SSD_D_INNER)), const((LANES, SSD_D_INNER)),
            const((1, SSD_D_INNER)), const((1, SSD_D_INNER)),
        ],
        out_specs=pl.BlockSpec((1, CHUNK, SSD_D_INNER), lambda b, c: (b, mb(c), 0)),
        out_shape=jax.ShapeDtypeStruct((nb, nc * CHUNK, SSD_D_INNER), BF16),
        scratch_shapes=[
            pltpu.VMEM((SSD_GROUPS, SSD_STATE, SSD_GW), F32),
            pltpu.VMEM((CHUNK, SSD_D_INNER), F32),
        ],
        compiler_params=pltpu.CompilerParams(
            dimension_semantics=("parallel", "arbitrary"), vmem_limit_bytes=VMEM_LIMIT),
        name="ssd_main",
    )(proj3, proj3, proj3, dt3, sb, dtbias, alog, e_f, e_b, dskip, nw)


def _merge_kernel(h_ref, yr_ref, ys_ref, gates_ref, wr_ref, ws_ref, wo_ref, o_ref):
    yr = _dot(yr_ref[...], wr_ref[...])
    ys = _dot(ys_ref[...], ws_ref[...])
    gr = _sigmoid(gates_ref[:, :D_MODEL].astype(F32))
    gs = _sigmoid(gates_ref[:, D_MODEL:].astype(F32))
    merged = (gr * yr + gs * ys).astype(BF16)
    o_ref[...] = h_ref[...] + _dot(merged, wo_ref[...])


def _merge_call(hm, yr, ys, proj, wr, ws, wo, lm):
    rows = hm.shape[0]
    tm = _row_tile(lm, 704)
    gw = 2 * D_MODEL
    return pl.pallas_call(
        _merge_kernel,
        grid=(rows // tm,),
        in_specs=[
            pl.BlockSpec((tm, D_MODEL), lambda i: (i, 0)),
            pl.BlockSpec((tm, RET_V), lambda i: (i, 0)),
            pl.BlockSpec((tm, SSD_D_INNER), lambda i: (i, 0)),
            pl.BlockSpec((tm, gw), lambda i: (i, COL_GATES // gw)),
            pl.BlockSpec((RET_V, D_MODEL), lambda i: (0, 0)),
            pl.BlockSpec((SSD_D_INNER, D_MODEL), lambda i: (0, 0)),
            pl.BlockSpec((D_MODEL, D_MODEL), lambda i: (0, 0)),
        ],
        out_specs=pl.BlockSpec((tm, D_MODEL), lambda i: (i, 0)),
        out_shape=jax.ShapeDtypeStruct((rows, D_MODEL), F32),
        compiler_params=pltpu.CompilerParams(
            dimension_semantics=("parallel",), vmem_limit_bytes=VMEM_LIMIT),
        name="merge_out",
    )(hm, yr, ys, proj, wr, ws, wo)


def _ffn_kernel(h_ref, hp_ref, hn_ref, nw_ref, wup_ref, cw_ref, cb_ref, wd_ref, fnw_ref, o_ref,
                u_ref, fg_ref, fu_ref, *, tm, tn):
    nw = nw_ref[...]
    u_ref[0:HALO, :] = _rms(hp_ref[0], nw).astype(BF16)
    u_ref[HALO:HALO + tm, :] = _rms(h_ref[0], nw).astype(BF16)
    u_ref[HALO + tm:2 * HALO + tm, :] = _rms(hn_ref[0], nw).astype(BF16)
    u = u_ref[...]

    def conv(f_ref, c0):
        return (cw_ref[0:1, c0:c0 + tn] * f_ref[HALO - 1:HALO - 1 + tm, :]
                + cw_ref[1:2, c0:c0 + tn] * f_ref[HALO:HALO + tm, :]
                + cw_ref[2:3, c0:c0 + tn] * f_ref[HALO + 1:HALO + 1 + tm, :] + cb_ref[:, c0:c0 + tn])

    y = h_ref[0]
    for j in range(D_FF // tn):
        fg_ref[...] = _dot(u, wup_ref[:, j * tn:(j + 1) * tn])
        fu_ref[...] = _dot(u, wup_ref[:, D_FF + j * tn:D_FF + (j + 1) * tn])
        act = (_silu(conv(fg_ref, j * tn)) * conv(fu_ref, D_FF + j * tn)).astype(BF16)
        y = y + _dot(act, wd_ref[j * tn:(j + 1) * tn, :])
    o_ref[0] = _rms(y, fnw_ref[...])


def _ffn_call(hmid3, nw, w_up, cw, cb, w_down, fnw, nb, s_len):
    lm = hmid3.shape[1]
    tm = 512 if s_len % 512 == 0 else CHUNK
    tn = 256
    nh = lm // HALO
    hpt = tm // HALO

    def prev_blk(t):
        return jnp.where(t == 0, nh - 1, t * hpt - 1)

    def resident(shape):
        return pl.BlockSpec(shape, lambda b, t: (0,) * len(shape), pipeline_mode=pl.Buffered(1))

    return pl.pallas_call(
        functools.partial(_ffn_kernel, tm=tm, tn=tn),
        grid=(nb, s_len // tm),
        in_specs=[
            pl.BlockSpec((1, tm, D_MODEL), lambda b, t: (b, t, 0)),
            pl.BlockSpec((1, HALO, D_MODEL), lambda b, t: (b, prev_blk(t), 0)),
            pl.BlockSpec((1, HALO, D_MODEL), lambda b, t: (b, (t + 1) * hpt, 0)),
            resident((1, D_MODEL)),
            resident((D_MODEL, 2 * D_FF)),
            resident((3, 2 * D_FF)),
            resident((1, 2 * D_FF)),
            resident((D_FF, D_MODEL)),
            resident((1, D_MODEL)),
        ],
        out_specs=pl.BlockSpec((1, tm, D_MODEL), lambda b, t: (b, t, 0)),
        out_shape=jax.ShapeDtypeStruct((nb, s_len, D_MODEL), F32),
        scratch_shapes=[
            pltpu.VMEM((tm + 2 * HALO, D_MODEL), BF16),
            pltpu.VMEM((tm + 2 * HALO, tn), F32),
            pltpu.VMEM((tm + 2 * HALO, tn), F32),
        ],
        compiler_params=pltpu.CompilerParams(
            dimension_semantics=("parallel", "parallel"), vmem_limit_bytes=VMEM_LIMIT),
        name="ffn",
    )(hmid3, hmid3, hmid3, nw, w_up, cw, cb, w_down, fnw)


def _position_tables(s_len):
    half = RET_QK_DIM // 2
    inv = ROPE_BASE ** (-jnp.arange(half, dtype=F32) / half)
    pos = jnp.concatenate([N_META + jnp.arange(s_len), jnp.zeros((PAD,), jnp.int32), jnp.arange(N_META)])
    ang = pos.astype(F32)[:, None] * inv[None, :]
    cos, sin = jnp.cos(ang), jnp.sin(ang)
    valid = jnp.concatenate([jnp.ones((s_len,), F32), jnp.zeros((PAD,), F32), jnp.ones((N_META,), F32)])
    return (jnp.concatenate([cos, cos], axis=1), jnp.concatenate([-sin, sin], axis=1),
            jnp.broadcast_to(valid[:, None], (s_len + CHUNK, LANES)))


def kernel(x, meta_tokens, norm_mix_w, w_in, ret_gn_w, w_ret_out, w_ssd_conv, b_ssd_conv, dt_bias_f, dt_bias_b, a_log_f, a_log_b, d_skip, ssd_norm_w, w_ssd_out, w_out, norm_ffn_w, w_ffn_up, w_ffn_conv, b_ffn_conv, w_ffn_down, final_norm_w):
    assert norm_mix_w.shape[0] == 1, "single-layer block"
    nb, s_len, _ = x.shape
    assert s_len % CHUNK == 0
    lm = s_len + CHUNK
    nc = lm // CHUNK

    tail = jnp.concatenate([jnp.zeros((PAD, D_MODEL), x.dtype), meta_tokens.astype(x.dtype)], axis=0)
    hm = jnp.concatenate([x, jnp.broadcast_to(tail[None], (nb, CHUNK, D_MODEL))], axis=1)

    wi = w_in[0]
    offs = np.cumsum([0, RET_QK, RET_QK, RET_V, RET_V, SSD_D_INNER, SSD_XBC, SSD_HEADS, SSD_HEADS, D_MODEL, D_MODEL])
    seg = [wi[:, offs[i]:offs[i + 1]] for i in range(10)]
    w_main = jnp.concatenate([seg[4], seg[5], seg[0], seg[1], seg[2], seg[3], seg[8], seg[9]], axis=1).astype(BF16)
    w_dt = jnp.concatenate([seg[6], seg[7], jnp.zeros((D_MODEL, DT_W - 2 * SSD_HEADS), wi.dtype)], axis=1).astype(BF16)
    cos, sin, valid = _position_tables(s_len)

    proj3, dt3 = _proj_call(hm, norm_mix_w, w_main, w_dt, cos, sin, valid, w_ssd_conv[0], b_ssd_conv)

    y_ret = _retention(proj3, ret_gn_w, nb, nc)

    lane_pad = jnp.zeros((1, DT_W - 2 * SSD_HEADS), F32)
    dtbias = jnp.concatenate([dt_bias_f, dt_bias_b, lane_pad], axis=1)
    alog = jnp.concatenate([a_log_f, a_log_b, lane_pad], axis=1)
    dskip = jnp.repeat(d_skip, SSD_HEAD_DIM, axis=1)
    y_ssd = _ssd(proj3, dt3, dtbias, alog, dskip, ssd_norm_w, nb, nc)

    h_mid = _merge_call(hm.reshape(nb * lm, D_MODEL), y_ret.reshape(nb * lm, RET_V),
                        y_ssd.reshape(nb * lm, SSD_D_INNER), proj3.reshape(nb * lm, PROJ_W),
                        w_ret_out[0].astype(BF16), w_ssd_out[0].astype(BF16), w_out[0].astype(BF16), lm)

    return _ffn_call(h_mid.reshape(nb, lm, D_MODEL), norm_ffn_w, w_ffn_up[0].astype(BF16), w_ffn_conv[0],
                     b_ffn_conv, w_ffn_down[0].astype(BF16), final_norm_w.reshape(1, D_MODEL), nb, s_len)
```

```python
import functools
import math

import jax
import jax.numpy as jnp
import numpy as np
from jax import lax
from jax.experimental import pallas as pl
from jax.experimental.pallas import tpu as pltpu

F32 = jnp.float32
BF16 = jnp.bfloat16

D_MODEL = 1024
N_META = 16
CHUNK = 128
PAD = CHUNK - N_META
RET_HEADS = 4
RET_QK_DIM = 128
RET_V_DIM = 256
RET_QK = RET_HEADS * RET_QK_DIM
RET_V = RET_HEADS * RET_V_DIM
SSD_D_INNER = 2 * D_MODEL
SSD_HEAD_DIM = 64
SSD_HEADS = SSD_D_INNER // SSD_HEAD_DIM
SSD_GROUPS = 4
SSD_HPG = SSD_HEADS // SSD_GROUPS
SSD_STATE = 128
SSD_GW = SSD_HPG * SSD_HEAD_DIM
SSD_BC = SSD_GROUPS * SSD_STATE
SSD_XBC = SSD_D_INNER + 2 * SSD_BC
D_FF = 2816
EPS = 1e-6
ROPE_BASE = 10000.0
LOG_GAMMA = tuple(math.log(1.0 - 2.0 ** (-5.0 - h)) for h in range(RET_HEADS))
LOG2E = math.log2(math.e)

COL_Z = 0
COL_XS = 2048
COL_B = 4096
COL_C = 4608
COL_Q = 5120
COL_K = 5632
COL_V = 6144
COL_G = 7168
COL_GATES = 8192
PROJ_W = 10240
PROJ_TN = 1024
QK_TILE = COL_Q // PROJ_TN
CONV_TILE0 = COL_XS // PROJ_TN
CONV_TILES = SSD_XBC // PROJ_TN
DT_W = 128

HALO = 16
LANES = 128
VMEM_LIMIT = 56 * 1024 * 1024


def _sigmoid(x):
    return 0.5 + 0.5 * jnp.tanh(0.5 * x)


def _silu(x):
    hx = 0.5 * x
    return hx + hx * jnp.tanh(hx)


def _softplus(x):
    return jnp.maximum(x, 0.0) + jnp.log1p(jnp.exp(-jnp.abs(x)))


def _split3(x):
    hi = x.astype(BF16)
    r1 = x - hi.astype(F32)
    mid = r1.astype(BF16)
    lo = (r1 - mid.astype(F32)).astype(BF16)
    return hi, mid, lo


def _dot(a, b):
    return jnp.dot(a, b, preferred_element_type=F32)


def _dot_tn(a, b):
    return lax.dot_general(a, b, (((0,), (0,)), ((), ())), preferred_element_type=F32)


def _dot_nt(a, b):
    return lax.dot_general(a, b, (((1,), (1,)), ((), ())), preferred_element_type=F32)


def _row_tile(rows, cap):
    best = CHUNK
    t = CHUNK
    while t <= min(rows, cap):
        if rows % t == 0:
            best = t
        t += CHUNK
    return best


def _rms(x, w):
    ms = jnp.mean(x * x, axis=-1, keepdims=True)
    return x * lax.rsqrt(ms + EPS) * w


def _conv_rows(f, w_ref, b_ref, cols, tm):
    prev = pltpu.roll(f, 1, 0)[HALO:HALO + tm]
    nxt = pltpu.roll(f, f.shape[0] - 1, 0)[HALO:HALO + tm]
    return (w_ref[0:1, cols] * prev + w_ref[1:2, cols] * f[HALO:HALO + tm] + w_ref[2:3, cols] * nxt
            + b_ref[:, cols])


def _proj_kernel(x_ref, xp_ref, xn_ref, nw_ref, w_ref, wdt_ref, cos_ref, sin_ref, valid_ref, cw_ref, cb_ref,
                 o_ref, odt_ref, u_ref, *, tm):
    nw = nw_ref[...]
    u_ref[0:HALO, :] = _rms(xp_ref[0], nw).astype(BF16)
    u_ref[HALO:HALO + tm, :] = _rms(x_ref[0], nw).astype(BF16)
    u_ref[HALO + tm:2 * HALO + tm, :] = _rms(xn_ref[0], nw).astype(BF16)
    u = u_ref[HALO:HALO + tm, :]
    odt_ref[0] = _dot(u, wdt_ref[...])
    cos = cos_ref[...]
    sin = sin_ref[...]
    valid = valid_ref[...]
    for j in range(PROJ_W // PROJ_TN):
        c0 = j * PROJ_TN
        if CONV_TILE0 <= j < CONV_TILE0 + CONV_TILES:
            f = _dot(u_ref[...], w_ref[:, c0:c0 + PROJ_TN])
            x0 = c0 - COL_XS
            act = _silu(_conv_rows(f, cw_ref, cb_ref, slice(x0, x0 + PROJ_TN), tm))
            for k in range(PROJ_TN // LANES):
                o_ref[0, :, c0 + k * LANES:c0 + (k + 1) * LANES] = (
                    act[:, k * LANES:(k + 1) * LANES] * valid).astype(BF16)
        elif j == QK_TILE:
            acc = _dot(u, w_ref[:, c0:c0 + PROJ_TN])
            for hb in range(2 * RET_HEADS):
                a = acc[:, hb * LANES:(hb + 1) * LANES]
                r = a * cos + pltpu.roll(a, RET_QK_DIM // 2, 1) * sin
                if hb >= RET_HEADS:
                    r = r * (RET_QK_DIM ** -0.5)
                o_ref[0, :, c0 + hb * LANES:c0 + (hb + 1) * LANES] = r.astype(BF16)
        else:
            o_ref[0, :, c0:c0 + PROJ_TN] = _dot(u, w_ref[:, c0:c0 + PROJ_TN]).astype(BF16)


def _proj_call(hm3, nw, w, wdt, cos, sin, valid, cw, cb):
    nb, lm, _ = hm3.shape
    tm = _row_tile(lm, 384)
    nh = lm // HALO
    hpt = tm // HALO

    def resident(shape):
        return pl.BlockSpec(shape, lambda b, t: (0,) * len(shape), pipeline_mode=pl.Buffered(1))

    return pl.pallas_call(
        functools.partial(_proj_kernel, tm=tm),
        grid=(nb, lm // tm),
        in_specs=[
            pl.BlockSpec((1, tm, D_MODEL), lambda b, t: (b, t, 0)),
            pl.BlockSpec((1, HALO, D_MODEL), lambda b, t: (b, lax.rem(t * hpt - 1 + nh, nh), 0)),
            pl.BlockSpec((1, HALO, D_MODEL), lambda b, t: (b, lax.rem((t + 1) * hpt, nh), 0)),
            resident((1, D_MODEL)),
            resident((D_MODEL, PROJ_W)),
            resident((D_MODEL, DT_W)),
            pl.BlockSpec((tm, LANES), lambda b, t: (t, 0)),
            pl.BlockSpec((tm, LANES), lambda b, t: (t, 0)),
            pl.BlockSpec((tm, LANES), lambda b, t: (t, 0)),
            resident((3, SSD_XBC)),
            resident((1, SSD_XBC)),
        ],
        out_specs=[
            pl.BlockSpec((1, tm, PROJ_W), lambda b, t: (b, t, 0)),
            pl.BlockSpec((1, tm, DT_W), lambda b, t: (b, t, 0)),
        ],
        out_shape=[
            jax.ShapeDtypeStruct((nb, lm, PROJ_W), BF16),
            jax.ShapeDtypeStruct((nb, lm, DT_W), F32),
        ],
        scratch_shapes=[pltpu.VMEM((tm + 2 * HALO, D_MODEL), BF16)],
        compiler_params=pltpu.CompilerParams(
            dimension_semantics=("parallel", "parallel"), vmem_limit_bytes=VMEM_LIMIT),
        name="in_proj",
    )(hm3, hm3, hm3, nw, w, wdt, cos, sin, valid, cw, cb)


def _chunk_pos():
    return lax.broadcasted_iota(jnp.int32, (CHUNK, 1), 0).astype(F32)


def _ret_bstate_kernel(k_ref, v_ref, o_ref, r_ref):
    @pl.when(pl.program_id(1) == 0)
    def _():
        r_ref[...] = jnp.zeros_like(r_ref)

    pos = _chunk_pos()
    for h in range(RET_HEADS):
        lg = LOG_GAMMA[h]
        r = r_ref[h]
        o_ref[0, 0, h * RET_QK_DIM:(h + 1) * RET_QK_DIM, :] = r.astype(BF16)
        k = k_ref[0, :, h * RET_QK_DIM:(h + 1) * RET_QK_DIM].astype(F32)
        kd = (k * jnp.exp(lg * pos)).astype(BF16)
        v = v_ref[0, :, h * RET_V_DIM:(h + 1) * RET_V_DIM]
        r_ref[h] = math.exp(CHUNK * lg) * r + _dot_tn(kd, v)


def _ret_main_kernel(q_ref, k_ref, v_ref, g_ref, rb_ref, gnw_ref, o_ref, r_ref):
    @pl.when(pl.program_id(1) == 0)
    def _():
        r_ref[...] = jnp.zeros_like(r_ref)

    pos = _chunk_pos()
    li = lax.broadcasted_iota(jnp.int32, (CHUNK, CHUNK), 0)
    si = lax.broadcasted_iota(jnp.int32, (CHUNK, CHUNK), 1)
    dist = jnp.abs(li - si).astype(F32)
    for h in range(RET_HEADS):
        lg = LOG_GAMMA[h]
        q = q_ref[0, :, h * RET_QK_DIM:(h + 1) * RET_QK_DIM]
        k = k_ref[0, :, h * RET_QK_DIM:(h + 1) * RET_QK_DIM]
        v = v_ref[0, :, h * RET_V_DIM:(h + 1) * RET_V_DIM]
        r = r_ref[h]
        s = _dot_nt(q, k) * jnp.exp(lg * dist)
        y = _dot(s.astype(BF16), v)
        y = y + jnp.exp(lg * (pos + 1.0)) * _dot(q, r.astype(BF16))
        rb = rb_ref[0, 0, h * RET_QK_DIM:(h + 1) * RET_QK_DIM, :]
        y = y + jnp.exp(lg * (CHUNK - pos)) * _dot(q, rb)
        kd = (k.astype(F32) * jnp.exp(lg * (CHUNK - 1.0 - pos))).astype(BF16)
        r_ref[h] = math.exp(CHUNK * lg) * r + _dot_tn(kd, v)
        mu = jnp.mean(y, axis=-1, keepdims=True)
        yc = y - mu
        var = jnp.mean(yc * yc, axis=-1, keepdims=True)
        yn = yc * lax.rsqrt(var + EPS)
        sl = slice(h * RET_V_DIM, (h + 1) * RET_V_DIM)
        g = g_ref[0, :, sl].astype(F32)
        o_ref[0, :, sl] = (_silu(g) * (yn * gnw_ref[:, sl])).astype(BF16)


def _mem_block(c, nr):
    return jnp.where(c == 0, nr, c - 1)


def _valid_rows(is_meta_chunk):
    row = lax.broadcasted_iota(jnp.int32, (CHUNK, 1), 0)
    return jnp.where(jnp.logical_and(is_meta_chunk, row < PAD), 0.0, 1.0).astype(F32)


def _ssd_decays(dt_ref, dtb_ref, alog_ref, valid):
    dt = _softplus(dt_ref[0] + dtb_ref[...]) * valid
    da = dt * (-LOG2E * jnp.exp(alog_ref[...]))
    li = lax.broadcasted_iota(jnp.int32, (CHUNK, CHUNK), 0)
    si = lax.broadcasted_iota(jnp.int32, (CHUNK, CHUNK), 1)
    tri = jnp.where(si <= li, 1.0, 0.0).astype(BF16)
    hi, mid, lo = _split3(da)
    acs = _dot(tri, hi) + _dot(tri, mid) + _dot(tri, lo)
    return dt, da, acs


def _pack_hi_lo(x):
    lane = lax.broadcasted_iota(jnp.int32, x.shape, 1)
    hi = x.astype(BF16).astype(F32)
    return jnp.where(lane < LANES // 2, hi, pltpu.roll(x - hi, LANES // 2, 1)).astype(BF16)


def _ssd_bstate_kernel(xs_ref, b_ref, dt_ref, dtb_ref, alog_ref, eb_ref, o_ref, s_ref):
    step = pl.program_id(1)

    @pl.when(step == 0)
    def _():
        s_ref[...] = jnp.zeros_like(s_ref)

    valid = _valid_rows(step == pl.num_programs(1) - 1)
    dt, da, acs = _ssd_decays(dt_ref, dtb_ref, alog_ref, valid)
    stack = jnp.concatenate([dt * jnp.exp2(acs - da), jnp.exp2(acs[CHUNK - HALO:CHUNK, :])], axis=0)
    ex = _dot(_pack_hi_lo(stack), eb_ref[...])
    xw = (xs_ref[0].astype(F32) * ex[0:CHUNK, :]).astype(BF16)
    cdec = ex[CHUNK + HALO - 1:CHUNK + HALO, :]
    bm = b_ref[0]
    for g in range(SSD_GROUPS):
        sl = slice(g * SSD_GW, (g + 1) * SSD_GW)
        s = s_ref[g]
        o_ref[0, 0, g * SSD_STATE:(g + 1) * SSD_STATE, :] = s.astype(BF16)
        s_ref[g] = cdec[:, sl] * s + _dot_tn(bm[:, g * SSD_STATE:(g + 1) * SSD_STATE], xw[:, sl])


def _pack3(x, src):
    lane = lax.broadcasted_iota(jnp.int32, x.shape, 1)
    hi = x.astype(BF16).astype(F32)
    r1 = x - hi
    mid = r1.astype(BF16).astype(F32)
    lo = r1 - mid
    packed = jnp.where(lane < src + SSD_HEADS, hi,
                       jnp.where(lane < src + 2 * SSD_HEADS, pltpu.roll(mid, SSD_HEADS, 1),
                                 pltpu.roll(lo, 2 * SSD_HEADS, 1)))
    return packed.astype(BF16)


def _ssd_main_kernel(z_ref, xs_ref, bc_ref, dt_ref, sb_ref, dtb_ref, alog_ref, ef_ref, eb_ref, cf_ref, cbk_ref,
                     dskip_ref, nw_ref, o_ref, s_ref, y_ref):
    step = pl.program_id(1)

    @pl.when(step == 0)
    def _():
        s_ref[...] = jnp.zeros_like(s_ref)

    valid = _valid_rows(step == 0)
    dt, da, acs = _ssd_decays(dt_ref, dtb_ref, alog_ref, valid)
    ecs = acs - da
    last = acs[CHUNK - 1:CHUNK, :]
    ldt_t = jnp.log2(dt).T
    row_f = acs.T - ldt_t
    row_b = ecs.T + ldt_t
    diag_t = jnp.log2(dt + pltpu.roll(dt, LANES - SSD_HEADS, 1)).T
    xs_b = xs_ref[0]
    bc = bc_ref[0]

    li = lax.broadcasted_iota(jnp.int32, (CHUNK, CHUNK), 0)
    si = lax.broadcasted_iota(jnp.int32, (CHUNK, CHUNK), 1)
    below = si < li
    above = si > li
    lane = lax.broadcasted_iota(jnp.int32, (CHUNK, LANES), 1)
    first_half = lane < SSD_HEAD_DIM
    zero = jnp.zeros((CHUNK, LANES), BF16)

    cb = []
    for g in range(SSD_GROUPS):
        bg = bc[:, g * SSD_STATE:(g + 1) * SSD_STATE]
        cg = bc[:, SSD_BC + g * SSD_STATE:SSD_BC + (g + 1) * SSD_STATE]
        cb.append(_dot_nt(cg, bg))

    col_f = _dot(_pack3(acs, 0), cf_ref[...])
    col_b = _dot(_pack3(ecs, SSD_HEADS), cbk_ref[...])

    def mix(h):
        lb = SSD_HEADS + h
        hs = slice(h * LANES, (h + 1) * LANES)
        causal = col_f[:, hs] - row_f[h:h + 1, :]
        anti = row_b[lb:lb + 1, :] - col_b[:, hs]
        e = jnp.where(below, causal, jnp.where(above, anti, diag_t[h:h + 1, :]))
        return (cb[h // SSD_HPG] * jnp.exp2(e)).astype(BF16)

    for p in range(SSD_HEADS // 2):
        xp = xs_b[:, p * LANES:(p + 1) * LANES]
        rhs = jnp.concatenate([jnp.where(first_half, xp, zero), jnp.where(first_half, zero, xp)], axis=0)
        y_ref[:, p * LANES:(p + 1) * LANES] = _dot(jnp.concatenate([mix(2 * p), mix(2 * p + 1)], axis=1), rhs)

    ea = jnp.exp2(acs)
    exf = _dot(jnp.concatenate([_pack_hi_lo(ea), _pack_hi_lo(dt * jnp.exp2(last - acs))], axis=0), ef_ref[...])
    ea_f = exf[0:CHUNK, :]
    w_f = exf[CHUNK:2 * CHUNK, :]
    ea_b = _dot(_pack_hi_lo(jnp.exp2(last - ecs)), eb_ref[...])
    cdec = ea_f[CHUNK - 1:CHUNK, :]
    xs = xs_b.astype(F32)
    xw = (xs * w_f).astype(BF16)
    for g in range(SSD_GROUPS):
        sl = slice(g * SSD_GW, (g + 1) * SSD_GW)
        bg = bc[:, g * SSD_STATE:(g + 1) * SSD_STATE]
        cg = bc[:, SSD_BC + g * SSD_STATE:SSD_BC + (g + 1) * SSD_STATE]
        s = s_ref[g]
        y_off = ea_f[:, sl] * _dot(cg, s.astype(BF16))
        y_off = y_off + ea_b[:, sl] * _dot(cg, sb_ref[0, 0, g * SSD_STATE:(g + 1) * SSD_STATE, :])
        s_ref[g] = cdec[:, sl] * s + _dot_tn(bg, xw[:, sl])
        y = y_ref[:, sl] + y_off + xs[:, sl] * dskip_ref[:, sl]
        y = y * _silu(z_ref[0, :, sl].astype(F32))
        o_ref[0, :, sl] = _rms(y, nw_ref[:, sl]).astype(BF16)


def _head_expander(offset):
    e = np.zeros((LANES, SSD_D_INNER), np.float32)
    for h in range(SSD_HEADS):
        e[offset + h, h * SSD_HEAD_DIM:(h + 1) * SSD_HEAD_DIM] = 1.0
        e[LANES // 2 + offset + h, h * SSD_HEAD_DIM:(h + 1) * SSD_HEAD_DIM] = 1.0
    return jnp.asarray(e, BF16)


def _lane_broadcaster(src):
    e = np.zeros((LANES, SSD_HEADS * LANES), np.float32)
    for h in range(SSD_HEADS):
        for part in range(3):
            e[src + part * SSD_HEADS + h, h * LANES:(h + 1) * LANES] = 1.0
    return jnp.asarray(e, BF16)


def _bstate_kernel(k_ref, v_ref, xs_ref, b_ref, dt_ref, dtb_ref, alog_ref, eb_ref, rb_ref, sb_ref, r_ref, s_ref):
    _ret_bstate_kernel(k_ref, v_ref, rb_ref, r_ref)
    _ssd_bstate_kernel(xs_ref, b_ref, dt_ref, dtb_ref, alog_ref, eb_ref, sb_ref, s_ref)


def _main_kernel(q_ref, k_ref, v_ref, g_ref, rb_ref, gnw_ref, z_ref, xs_ref, bc_ref, dt_ref, sb_ref, dtb_ref,
                 alog_ref, ef_ref, eb_ref, cf_ref, cbk_ref, dskip_ref, nw_ref, yr_ref, ys_ref, r_ref, s_ref, y_ref):
    _ret_main_kernel(q_ref, k_ref, v_ref, g_ref, rb_ref, gnw_ref, yr_ref, r_ref)
    _ssd_main_kernel(z_ref, xs_ref, bc_ref, dt_ref, sb_ref, dtb_ref, alog_ref, ef_ref, eb_ref, cf_ref, cbk_ref,
                     dskip_ref, nw_ref, ys_ref, s_ref, y_ref)


def _mixers(proj3, dt3, gnw, dtbias, alog, dskip, nw, nb, nc):
    nr = nc - 1
    qb, vb = RET_QK, RET_V
    xw_, bw_, bcw_ = SSD_D_INNER, SSD_BC, 2 * SSD_BC
    e_f = _head_expander(0)
    e_b = _head_expander(SSD_HEADS)
    c_f = _lane_broadcaster(0)
    c_b = _lane_broadcaster(SSD_HEADS)
    params = pltpu.CompilerParams(dimension_semantics=("parallel", "arbitrary"), vmem_limit_bytes=VMEM_LIMIT)

    def const(shape):
        return pl.BlockSpec(shape, lambda b, s: (0,) * len(shape))

    def rev(s):
        return _mem_block(nc - 1 - s, nr)

    rb, sb = pl.pallas_call(
        _bstate_kernel,
        grid=(nb, nc),
        in_specs=[
            pl.BlockSpec((1, CHUNK, qb), lambda b, s: (b, rev(s), COL_K // qb)),
            pl.BlockSpec((1, CHUNK, vb), lambda b, s: (b, rev(s), COL_V // vb)),
            pl.BlockSpec((1, CHUNK, xw_), lambda b, s: (b, rev(s), COL_XS // xw_)),
            pl.BlockSpec((1, CHUNK, bw_), lambda b, s: (b, rev(s), COL_B // bw_)),
            pl.BlockSpec((1, CHUNK, DT_W), lambda b, s: (b, rev(s), 0)),
            const((1, DT_W)), const((1, DT_W)), const((LANES, SSD_D_INNER)),
        ],
        out_specs=[
            pl.BlockSpec((1, 1, RET_QK, RET_V_DIM), lambda b, s: (b, nc - 1 - s, 0, 0)),
            pl.BlockSpec((1, 1, SSD_BC, SSD_GW), lambda b, s: (b, nc - 1 - s, 0, 0)),
        ],
        out_shape=[
            jax.ShapeDtypeStruct((nb, nc, RET_QK, RET_V_DIM), BF16),
            jax.ShapeDtypeStruct((nb, nc, SSD_BC, SSD_GW), BF16),
        ],
        scratch_shapes=[
            pltpu.VMEM((RET_HEADS, RET_QK_DIM, RET_V_DIM), F32),
            pltpu.VMEM((SSD_GROUPS, SSD_STATE, SSD_GW), F32),
        ],
        compiler_params=params,
        name="mix_bwd_state",
    )(proj3, proj3, proj3, proj3, dt3, dtbias, alog, e_b)

    def mb(c):
        return _mem_block(c, nr)

    return pl.pallas_call(
        _main_kernel,
        grid=(nb, nc),
        in_specs=[
            pl.BlockSpec((1, CHUNK, qb), lambda b, c: (b, mb(c), COL_Q // qb)),
            pl.BlockSpec((1, CHUNK, qb), lambda b, c: (b, mb(c), COL_K // qb)),
            pl.BlockSpec((1, CHUNK, vb), lambda b, c: (b, mb(c), COL_V // vb)),
            pl.BlockSpec((1, CHUNK, vb), lambda b, c: (b, mb(c), COL_G // vb)),
            pl.BlockSpec((1, 1, RET_QK, RET_V_DIM), lambda b, c: (b, c, 0, 0)),
            const((1, RET_V)),
            pl.BlockSpec((1, CHUNK, xw_), lambda b, c: (b, mb(c), COL_Z // xw_)),
            pl.BlockSpec((1, CHUNK, xw_), lambda b, c: (b, mb(c), COL_XS // xw_)),
            pl.BlockSpec((1, CHUNK, bcw_), lambda b, c: (b, mb(c), COL_B // bcw_)),
            pl.BlockSpec((1, CHUNK, DT_W), lambda b, c: (b, mb(c), 0)),
            pl.BlockSpec((1, 1, SSD_BC, SSD_GW), lambda b, c: (b, c, 0, 0)),
            const((1, DT_W)), const((1, DT_W)),
            const((LANES, SSD_D_INNER)), const((LANES, SSD_D_INNER)),
            const((LANES, SSD_HEADS * LANES)), const((LANES, SSD_HEADS * LANES)),
            const((1, SSD_D_INNER)), const((1, SSD_D_INNER)),
        ],
        out_specs=[
            pl.BlockSpec((1, CHUNK, RET_V), lambda b, c: (b, mb(c), 0)),
            pl.BlockSpec((1, CHUNK, SSD_D_INNER), lambda b, c: (b, mb(c), 0)),
        ],
        out_shape=[
            jax.ShapeDtypeStruct((nb, nc * CHUNK, RET_V), BF16),
            jax.ShapeDtypeStruct((nb, nc * CHUNK, SSD_D_INNER), BF16),
        ],
        scratch_shapes=[
            pltpu.VMEM((RET_HEADS, RET_QK_DIM, RET_V_DIM), F32),
            pltpu.VMEM((SSD_GROUPS, SSD_STATE, SSD_GW), F32),
            pltpu.VMEM((CHUNK, SSD_D_INNER), F32),
        ],
        compiler_params=params,
        name="mix_main",
    )(proj3, proj3, proj3, proj3, rb, gnw, proj3, proj3, proj3, dt3, sb, dtbias, alog, e_f, e_b, c_f, c_b, dskip, nw)


def _merge_kernel(h_ref, yr_ref, ys_ref, gates_ref, wr_ref, ws_ref, wo_ref, o_ref):
    yr = _dot(yr_ref[...], wr_ref[...])
    ys = _dot(ys_ref[...], ws_ref[...])
    gr = _sigmoid(gates_ref[:, :D_MODEL].astype(F32))
    gs = _sigmoid(gates_ref[:, D_MODEL:].astype(F32))
    merged = (gr * yr + gs * ys).astype(BF16)
    o_ref[...] = h_ref[...] + _dot(merged, wo_ref[...])


def _merge_call(hm, yr, ys, proj, wr, ws, wo, lm):
    rows = hm.shape[0]
    tm = _row_tile(lm, 704)
    gw = 2 * D_MODEL
    return pl.pallas_call(
        _merge_kernel,
        grid=(rows // tm,),
        in_specs=[
            pl.BlockSpec((tm, D_MODEL), lambda i: (i, 0)),
            pl.BlockSpec((tm, RET_V), lambda i: (i, 0)),
            pl.BlockSpec((tm, SSD_D_INNER), lambda i: (i, 0)),
            pl.BlockSpec((tm, gw), lambda i: (i, COL_GATES // gw)),
            pl.BlockSpec((RET_V, D_MODEL), lambda i: (0, 0)),
            pl.BlockSpec((SSD_D_INNER, D_MODEL), lambda i: (0, 0)),
            pl.BlockSpec((D_MODEL, D_MODEL), lambda i: (0, 0)),
        ],
        out_specs=pl.BlockSpec((tm, D_MODEL), lambda i: (i, 0)),
        out_shape=jax.ShapeDtypeStruct((rows, D_MODEL), F32),
        compiler_params=pltpu.CompilerParams(
            dimension_semantics=("parallel",), vmem_limit_bytes=VMEM_LIMIT),
        name="merge_out",
    )(hm, yr, ys, proj, wr, ws, wo)


def _ffn_kernel(h_ref, hp_ref, hn_ref, nw_ref, wup_ref, cw_ref, cb_ref, wd_ref, fnw_ref, o_ref, u_ref,
                *, tm, tn):
    nw = nw_ref[...]
    u_ref[0:HALO, :] = _rms(hp_ref[0], nw).astype(BF16)
    u_ref[HALO:HALO + tm, :] = _rms(h_ref[0], nw).astype(BF16)
    u_ref[HALO + tm:2 * HALO + tm, :] = _rms(hn_ref[0], nw).astype(BF16)
    u = u_ref[...]

    def up(j):
        return (_dot(u, wup_ref[:, j * tn:(j + 1) * tn]),
                _dot(u, wup_ref[:, D_FF + j * tn:D_FF + (j + 1) * tn]))

    y = h_ref[0]
    nxt = up(0)
    for j in range(D_FF // tn):
        fg, fu = nxt
        if j + 1 < D_FF // tn:
            nxt = up(j + 1)
        gc = slice(j * tn, (j + 1) * tn)
        uc = slice(D_FF + j * tn, D_FF + (j + 1) * tn)
        act = _silu(_conv_rows(fg, cw_ref, cb_ref, gc, tm)) * _conv_rows(fu, cw_ref, cb_ref, uc, tm)
        y = y + _dot(act.astype(BF16), wd_ref[gc, :])
    o_ref[0] = _rms(y, fnw_ref[...])


def _ffn_call(hmid3, nw, w_up, cw, cb, w_down, fnw, nb, s_len):
    lm = hmid3.shape[1]
    tm = 512 if s_len % 512 == 0 else CHUNK
    tn = 256
    nh = lm // HALO
    hpt = tm // HALO

    def prev_blk(t):
        return jnp.where(t == 0, nh - 1, t * hpt - 1)

    def resident(shape):
        return pl.BlockSpec(shape, lambda b, t: (0,) * len(shape), pipeline_mode=pl.Buffered(1))

    return pl.pallas_call(
        functools.partial(_ffn_kernel, tm=tm, tn=tn),
        grid=(nb, s_len // tm),
        in_specs=[
            pl.BlockSpec((1, tm, D_MODEL), lambda b, t: (b, t, 0)),
            pl.BlockSpec((1, HALO, D_MODEL), lambda b, t: (b, prev_blk(t), 0)),
            pl.BlockSpec((1, HALO, D_MODEL), lambda b, t: (b, (t + 1) * hpt, 0)),
            resident((1, D_MODEL)),
            resident((D_MODEL, 2 * D_FF)),
            resident((3, 2 * D_FF)),
            resident((1, 2 * D_FF)),
            resident((D_FF, D_MODEL)),
            resident((1, D_MODEL)),
        ],
        out_specs=pl.BlockSpec((1, tm, D_MODEL), lambda b, t: (b, t, 0)),
        out_shape=jax.ShapeDtypeStruct((nb, s_len, D_MODEL), F32),
        scratch_shapes=[pltpu.VMEM((tm + 2 * HALO, D_MODEL), BF16)],
        compiler_params=pltpu.CompilerParams(
            dimension_semantics=("parallel", "parallel"), vmem_limit_bytes=VMEM_LIMIT),
        name="ffn",
    )(hmid3, hmid3, hmid3, nw, w_up, cw, cb, w_down, fnw)


def _position_tables(s_len):
    half = RET_QK_DIM // 2
    inv = ROPE_BASE ** (-jnp.arange(half, dtype=F32) / half)
    pos = jnp.concatenate([N_META + jnp.arange(s_len), jnp.zeros((PAD,), jnp.int32), jnp.arange(N_META)])
    ang = pos.astype(F32)[:, None] * inv[None, :]
    cos, sin = jnp.cos(ang), jnp.sin(ang)
    valid = jnp.concatenate([jnp.ones((s_len,), F32), jnp.zeros((PAD,), F32), jnp.ones((N_META,), F32)])
    return (jnp.concatenate([cos, cos], axis=1), jnp.concatenate([-sin, sin], axis=1),
            jnp.broadcast_to(valid[:, None], (s_len + CHUNK, LANES)))


def kernel(x, meta_tokens, norm_mix_w, w_in, ret_gn_w, w_ret_out, w_ssd_conv, b_ssd_conv, dt_bias_f, dt_bias_b, a_log_f, a_log_b, d_skip, ssd_norm_w, w_ssd_out, w_out, norm_ffn_w, w_ffn_up, w_ffn_conv, b_ffn_conv, w_ffn_down, final_norm_w):
    assert norm_mix_w.shape[0] == 1, "single-layer block"
    nb, s_len, _ = x.shape
    assert s_len % CHUNK == 0
    lm = s_len + CHUNK
    nc = lm // CHUNK

    tail = jnp.concatenate([jnp.zeros((PAD, D_MODEL), x.dtype), meta_tokens.astype(x.dtype)], axis=0)
    hm = jnp.concatenate([x, jnp.broadcast_to(tail[None], (nb, CHUNK, D_MODEL))], axis=1)

    wi = w_in[0]
    offs = np.cumsum([0, RET_QK, RET_QK, RET_V, RET_V, SSD_D_INNER, SSD_XBC, SSD_HEADS, SSD_HEADS, D_MODEL, D_MODEL])
    seg = [wi[:, offs[i]:offs[i + 1]] for i in range(10)]
    w_main = jnp.concatenate([seg[4], seg[5], seg[0], seg[1], seg[2], seg[3], seg[8], seg[9]], axis=1).astype(BF16)
    w_dt = jnp.concatenate([seg[6], seg[7], jnp.zeros((D_MODEL, DT_W - 2 * SSD_HEADS), wi.dtype)], axis=1).astype(BF16)
    cos, sin, valid = _position_tables(s_len)

    proj3, dt3 = _proj_call(hm, norm_mix_w, w_main, w_dt, cos, sin, valid, w_ssd_conv[0], b_ssd_conv)

    lane_pad = jnp.zeros((1, DT_W - 2 * SSD_HEADS), F32)
    dtbias = jnp.concatenate([dt_bias_f, dt_bias_b, lane_pad], axis=1)
    alog = jnp.concatenate([a_log_f, a_log_b, lane_pad], axis=1)
    dskip = jnp.repeat(d_skip, SSD_HEAD_DIM, axis=1)
    y_ret, y_ssd = _mixers(proj3, dt3, ret_gn_w, dtbias, alog, dskip, ssd_norm_w, nb, nc)

    h_mid = _merge_call(hm.reshape(nb * lm, D_MODEL), y_ret.reshape(nb * lm, RET_V),
                        y_ssd.reshape(nb * lm, SSD_D_INNER), proj3.reshape(nb * lm, PROJ_W),
                        w_ret_out[0].astype(BF16), w_ssd_out[0].astype(BF16), w_out[0].astype(BF16), lm)

    return _ffn_call(h_mid.reshape(nb, lm, D_MODEL), norm_ffn_w, w_ffn_up[0].astype(BF16), w_ffn_conv[0],
                     b_ffn_conv, w_ffn_down[0].astype(BF16), final_norm_w.reshape(1, D_MODEL), nb, s_len)
```

```python
import functools
import math

import jax
import jax.numpy as jnp
import numpy as np
from jax import lax
from jax.experimental import pallas as pl
from jax.experimental.pallas import tpu as pltpu

F32 = jnp.float32
BF16 = jnp.bfloat16

D_MODEL = 1024
N_META = 16
CHUNK = 128
PAD = CHUNK - N_META
RET_HEADS = 4
RET_QK_DIM = 128
RET_V_DIM = 256
RET_QK = RET_HEADS * RET_QK_DIM
RET_V = RET_HEADS * RET_V_DIM
SSD_D_INNER = 2 * D_MODEL
SSD_HEAD_DIM = 64
SSD_HEADS = SSD_D_INNER // SSD_HEAD_DIM
SSD_GROUPS = 4
SSD_HPG = SSD_HEADS // SSD_GROUPS
SSD_STATE = 128
SSD_GW = SSD_HPG * SSD_HEAD_DIM
SSD_BC = SSD_GROUPS * SSD_STATE
SSD_XBC = SSD_D_INNER + 2 * SSD_BC
D_FF = 2816
EPS = 1e-6
ROPE_BASE = 10000.0
LOG_GAMMA = tuple(math.log(1.0 - 2.0 ** (-5.0 - h)) for h in range(RET_HEADS))
LOG2E = math.log2(math.e)

COL_Z = 0
COL_XS = 2048
COL_B = 4096
COL_C = 4608
COL_Q = 5120
COL_K = 5632
COL_V = 6144
COL_G = 7168
COL_GATES = 8192
PROJ_W = 10240
PROJ_TN = 1024
QK_TILE = COL_Q // PROJ_TN
CONV_TILE0 = COL_XS // PROJ_TN
CONV_TILES = SSD_XBC // PROJ_TN
DT_W = 128

HALO = 16
LANES = 128
VMEM_LIMIT = 56 * 1024 * 1024


def _sigmoid(x):
    return 0.5 + 0.5 * jnp.tanh(0.5 * x)


def _silu(x):
    hx = 0.5 * x
    return hx + hx * jnp.tanh(hx)


def _softplus(x):
    return jnp.maximum(x, 0.0) + jnp.log1p(jnp.exp(-jnp.abs(x)))


def _split3(x):
    hi = x.astype(BF16)
    r1 = x - hi.astype(F32)
    mid = r1.astype(BF16)
    lo = (r1 - mid.astype(F32)).astype(BF16)
    return hi, mid, lo


def _dot(a, b):
    return jnp.dot(a, b, preferred_element_type=F32)


def _dot_tn(a, b):
    return lax.dot_general(a, b, (((0,), (0,)), ((), ())), preferred_element_type=F32)


def _dot_nt(a, b):
    return lax.dot_general(a, b, (((1,), (1,)), ((), ())), preferred_element_type=F32)


def _row_tile(rows, cap):
    best = CHUNK
    t = CHUNK
    while t <= min(rows, cap):
        if rows % t == 0:
            best = t
        t += CHUNK
    return best


def _rms(x, w):
    ms = jnp.mean(x * x, axis=-1, keepdims=True)
    return x * lax.rsqrt(ms + EPS) * w


def _conv_rows(f, w_ref, b_ref, cols, tm):
    prev = pltpu.roll(f, 1, 0)[HALO:HALO + tm]
    nxt = pltpu.roll(f, f.shape[0] - 1, 0)[HALO:HALO + tm]
    return (w_ref[0:1, cols] * prev + w_ref[1:2, cols] * f[HALO:HALO + tm] + w_ref[2:3, cols] * nxt
            + b_ref[:, cols])


def _proj_kernel(x_ref, xp_ref, xn_ref, nw_ref, w_ref, wdt_ref, cos_ref, sin_ref, valid_ref, cw_ref, cb_ref,
                 o_ref, odt_ref, u_ref, *, tm):
    nw = nw_ref[...]
    u_ref[0:HALO, :] = _rms(xp_ref[0], nw).astype(BF16)
    u_ref[HALO:HALO + tm, :] = _rms(x_ref[0], nw).astype(BF16)
    u_ref[HALO + tm:2 * HALO + tm, :] = _rms(xn_ref[0], nw).astype(BF16)
    u = u_ref[HALO:HALO + tm, :]
    odt_ref[0] = _dot(u, wdt_ref[...])
    cos = cos_ref[...]
    sin = sin_ref[...]
    valid = valid_ref[...]
    for j in range(PROJ_W // PROJ_TN):
        c0 = j * PROJ_TN
        if CONV_TILE0 <= j < CONV_TILE0 + CONV_TILES:
            f = _dot(u_ref[...], w_ref[:, c0:c0 + PROJ_TN])
            x0 = c0 - COL_XS
            act = _silu(_conv_rows(f, cw_ref, cb_ref, slice(x0, x0 + PROJ_TN), tm))
            for k in range(PROJ_TN // LANES):
                o_ref[0, :, c0 + k * LANES:c0 + (k + 1) * LANES] = (
                    act[:, k * LANES:(k + 1) * LANES] * valid).astype(BF16)
        elif j == QK_TILE:
            acc = _dot(u, w_ref[:, c0:c0 + PROJ_TN])
            for hb in range(2 * RET_HEADS):
                a = acc[:, hb * LANES:(hb + 1) * LANES]
                r = a * cos + pltpu.roll(a, RET_QK_DIM // 2, 1) * sin
                if hb >= RET_HEADS:
                    r = r * (RET_QK_DIM ** -0.5)
                o_ref[0, :, c0 + hb * LANES:c0 + (hb + 1) * LANES] = r.astype(BF16)
        else:
            o_ref[0, :, c0:c0 + PROJ_TN] = _dot(u, w_ref[:, c0:c0 + PROJ_TN]).astype(BF16)


def _proj_call(hm3, nw, w, wdt, cos, sin, valid, cw, cb):
    nb, lm, _ = hm3.shape
    tm = _row_tile(lm, 384)
    nh = lm // HALO
    hpt = tm // HALO

    def resident(shape):
        return pl.BlockSpec(shape, lambda b, t: (0,) * len(shape), pipeline_mode=pl.Buffered(1))

    return pl.pallas_call(
        functools.partial(_proj_kernel, tm=tm),
        grid=(nb, lm // tm),
        in_specs=[
            pl.BlockSpec((1, tm, D_MODEL), lambda b, t: (b, t, 0)),
            pl.BlockSpec((1, HALO, D_MODEL), lambda b, t: (b, lax.rem(t * hpt - 1 + nh, nh), 0)),
            pl.BlockSpec((1, HALO, D_MODEL), lambda b, t: (b, lax.rem((t + 1) * hpt, nh), 0)),
            resident((1, D_MODEL)),
            resident((D_MODEL, PROJ_W)),
            resident((D_MODEL, DT_W)),
            pl.BlockSpec((tm, LANES), lambda b, t: (t, 0)),
            pl.BlockSpec((tm, LANES), lambda b, t: (t, 0)),
            pl.BlockSpec((tm, LANES), lambda b, t: (t, 0)),
            resident((3, SSD_XBC)),
            resident((1, SSD_XBC)),
        ],
        out_specs=[
            pl.BlockSpec((1, tm, PROJ_W), lambda b, t: (b, t, 0)),
            pl.BlockSpec((1, tm, DT_W), lambda b, t: (b, t, 0)),
        ],
        out_shape=[
            jax.ShapeDtypeStruct((nb, lm, PROJ_W), BF16),
            jax.ShapeDtypeStruct((nb, lm, DT_W), F32),
        ],
        scratch_shapes=[pltpu.VMEM((tm + 2 * HALO, D_MODEL), BF16)],
        compiler_params=pltpu.CompilerParams(
            dimension_semantics=("parallel", "parallel"), vmem_limit_bytes=VMEM_LIMIT),
        name="in_proj",
    )(hm3, hm3, hm3, nw, w, wdt, cos, sin, valid, cw, cb)


def _chunk_pos():
    return lax.broadcasted_iota(jnp.int32, (CHUNK, 1), 0).astype(F32)


def _ret_bstate_chunk(k_ref, v_ref, o_ref, r_ref):
    pos = _chunk_pos()
    for h in range(RET_HEADS):
        lg = LOG_GAMMA[h]
        r = r_ref[h]
        o_ref[h * RET_QK_DIM:(h + 1) * RET_QK_DIM, :] = r.astype(BF16)
        k = k_ref[:, h * RET_QK_DIM:(h + 1) * RET_QK_DIM].astype(F32)
        kd = (k * jnp.exp(lg * pos)).astype(BF16)
        v = v_ref[:, h * RET_V_DIM:(h + 1) * RET_V_DIM]
        r_ref[h] = math.exp(CHUNK * lg) * r + _dot_tn(kd, v)


def _ret_main_chunk(q_ref, k_ref, v_ref, g_ref, rb_ref, gnw_ref, o_ref, r_ref):
    pos = _chunk_pos()
    li = lax.broadcasted_iota(jnp.int32, (CHUNK, CHUNK), 0)
    si = lax.broadcasted_iota(jnp.int32, (CHUNK, CHUNK), 1)
    dist = jnp.abs(li - si).astype(F32)
    for h in range(RET_HEADS):
        lg = LOG_GAMMA[h]
        q = q_ref[:, h * RET_QK_DIM:(h + 1) * RET_QK_DIM]
        k = k_ref[:, h * RET_QK_DIM:(h + 1) * RET_QK_DIM]
        v = v_ref[:, h * RET_V_DIM:(h + 1) * RET_V_DIM]
        r = r_ref[h]
        s = _dot_nt(q, k) * jnp.exp(lg * dist)
        y = _dot(s.astype(BF16), v)
        y = y + jnp.exp(lg * (pos + 1.0)) * _dot(q, r.astype(BF16))
        rb = rb_ref[h * RET_QK_DIM:(h + 1) * RET_QK_DIM, :]
        y = y + jnp.exp(lg * (CHUNK - pos)) * _dot(q, rb)
        kd = (k.astype(F32) * jnp.exp(lg * (CHUNK - 1.0 - pos))).astype(BF16)
        r_ref[h] = math.exp(CHUNK * lg) * r + _dot_tn(kd, v)
        mu = jnp.mean(y, axis=-1, keepdims=True)
        yc = y - mu
        var = jnp.mean(yc * yc, axis=-1, keepdims=True)
        yn = yc * lax.rsqrt(var + EPS)
        sl = slice(h * RET_V_DIM, (h + 1) * RET_V_DIM)
        g = g_ref[:, sl].astype(F32)
        o_ref[:, sl] = (_silu(g) * (yn * gnw_ref[:, sl])).astype(BF16)


def _mem_block(c, nr):
    return jnp.where(c == 0, nr, c - 1)


def _valid_rows(is_meta_chunk):
    row = lax.broadcasted_iota(jnp.int32, (CHUNK, 1), 0)
    return jnp.where(jnp.logical_and(is_meta_chunk, row < PAD), 0.0, 1.0).astype(F32)


def _ssd_decays(dt_ref, dtb_ref, alog_ref, valid):
    dt = _softplus(dt_ref[...] + dtb_ref[...]) * valid
    da = dt * (-LOG2E * jnp.exp(alog_ref[...]))
    li = lax.broadcasted_iota(jnp.int32, (CHUNK, CHUNK), 0)
    si = lax.broadcasted_iota(jnp.int32, (CHUNK, CHUNK), 1)
    tri = jnp.where(si <= li, 1.0, 0.0).astype(BF16)
    hi, mid, lo = _split3(da)
    acs = _dot(tri, hi) + _dot(tri, mid) + _dot(tri, lo)
    return dt, da, acs


def _pack_hi_lo(x):
    lane = lax.broadcasted_iota(jnp.int32, x.shape, 1)
    hi = x.astype(BF16).astype(F32)
    return jnp.where(lane < LANES // 2, hi, pltpu.roll(x - hi, LANES // 2, 1)).astype(BF16)


def _ssd_bstate_chunk(xs_ref, b_ref, dt_ref, dtb_ref, alog_ref, eb_ref, o_ref, s_ref, valid):
    dt, da, acs = _ssd_decays(dt_ref, dtb_ref, alog_ref, valid)
    stack = jnp.concatenate([dt * jnp.exp2(acs - da), jnp.exp2(acs[CHUNK - HALO:CHUNK, :])], axis=0)
    ex = _dot(_pack_hi_lo(stack), eb_ref[...])
    xw = (xs_ref[...].astype(F32) * ex[0:CHUNK, :]).astype(BF16)
    cdec = ex[CHUNK + HALO - 1:CHUNK + HALO, :]
    bm = b_ref[...]
    for g in range(SSD_GROUPS):
        sl = slice(g * SSD_GW, (g + 1) * SSD_GW)
        s = s_ref[g]
        o_ref[g * SSD_STATE:(g + 1) * SSD_STATE, :] = s.astype(BF16)
        s_ref[g] = cdec[:, sl] * s + _dot_tn(bm[:, g * SSD_STATE:(g + 1) * SSD_STATE], xw[:, sl])


def _pack3(x, src):
    lane = lax.broadcasted_iota(jnp.int32, x.shape, 1)
    hi = x.astype(BF16).astype(F32)
    r1 = x - hi
    mid = r1.astype(BF16).astype(F32)
    lo = r1 - mid
    packed = jnp.where(lane < src + SSD_HEADS, hi,
                       jnp.where(lane < src + 2 * SSD_HEADS, pltpu.roll(mid, SSD_HEADS, 1),
                                 pltpu.roll(lo, 2 * SSD_HEADS, 1)))
    return packed.astype(BF16)


def _ssd_main_chunk(z_ref, xs_ref, bc_ref, dt_ref, sb_ref, dtb_ref, alog_ref, ef_ref, eb_ref, cf_ref, cbk_ref,
                    dskip_ref, nw_ref, o_ref, s_ref, y_ref, valid):
    dt, da, acs = _ssd_decays(dt_ref, dtb_ref, alog_ref, valid)
    ecs = acs - da
    last = acs[CHUNK - 1:CHUNK, :]
    ldt_t = jnp.log2(dt).T
    row_f = acs.T - ldt_t
    row_b = ecs.T + ldt_t
    diag_t = jnp.log2(dt + pltpu.roll(dt, LANES - SSD_HEADS, 1)).T
    xs_b = xs_ref[...]
    bc = bc_ref[...]

    li = lax.broadcasted_iota(jnp.int32, (CHUNK, CHUNK), 0)
    si = lax.broadcasted_iota(jnp.int32, (CHUNK, CHUNK), 1)
    below = si < li
    above = si > li
    lane = lax.broadcasted_iota(jnp.int32, (CHUNK, LANES), 1)
    first_half = lane < SSD_HEAD_DIM
    zero = jnp.zeros((CHUNK, LANES), BF16)

    cb = []
    for g in range(SSD_GROUPS):
        bg = bc[:, g * SSD_STATE:(g + 1) * SSD_STATE]
        cg = bc[:, SSD_BC + g * SSD_STATE:SSD_BC + (g + 1) * SSD_STATE]
        cb.append(_dot_nt(cg, bg))

    col_f = _dot(_pack3(acs, 0), cf_ref[...])
    col_b = _dot(_pack3(ecs, SSD_HEADS), cbk_ref[...])

    def mix(h):
        lb = SSD_HEADS + h
        hs = slice(h * LANES, (h + 1) * LANES)
        causal = col_f[:, hs] - row_f[h:h + 1, :]
        anti = row_b[lb:lb + 1, :] - col_b[:, hs]
        e = jnp.where(below, causal, jnp.where(above, anti, diag_t[h:h + 1, :]))
        return (cb[h // SSD_HPG] * jnp.exp2(e)).astype(BF16)

    for p in range(SSD_HEADS // 2):
        xp = xs_b[:, p * LANES:(p + 1) * LANES]
        rhs = jnp.concatenate([jnp.where(first_half, xp, zero), jnp.where(first_half, zero, xp)], axis=0)
        y_ref[:, p * LANES:(p + 1) * LANES] = _dot(jnp.concatenate([mix(2 * p), mix(2 * p + 1)], axis=1), rhs)

    ea = jnp.exp2(acs)
    exf = _dot(jnp.concatenate([_pack_hi_lo(ea), _pack_hi_lo(dt * jnp.exp2(last - acs))], axis=0), ef_ref[...])
    ea_f = exf[0:CHUNK, :]
    w_f = exf[CHUNK:2 * CHUNK, :]
    ea_b = _dot(_pack_hi_lo(jnp.exp2(last - ecs)), eb_ref[...])
    cdec = ea_f[CHUNK - 1:CHUNK, :]
    xs = xs_b.astype(F32)
    xw = (xs * w_f).astype(BF16)
    for g in range(SSD_GROUPS):
        sl = slice(g * SSD_GW, (g + 1) * SSD_GW)
        bg = bc[:, g * SSD_STATE:(g + 1) * SSD_STATE]
        cg = bc[:, SSD_BC + g * SSD_STATE:SSD_BC + (g + 1) * SSD_STATE]
        s = s_ref[g]
        y_off = ea_f[:, sl] * _dot(cg, s.astype(BF16))
        y_off = y_off + ea_b[:, sl] * _dot(cg, sb_ref[g * SSD_STATE:(g + 1) * SSD_STATE, :])
        s_ref[g] = cdec[:, sl] * s + _dot_tn(bg, xw[:, sl])
        y = y_ref[:, sl] + y_off + xs[:, sl] * dskip_ref[:, sl]
        y = y * _silu(z_ref[:, sl].astype(F32))
        o_ref[:, sl] = _rms(y, nw_ref[:, sl]).astype(BF16)


def _head_expander(offset):
    e = np.zeros((LANES, SSD_D_INNER), np.float32)
    for h in range(SSD_HEADS):
        e[offset + h, h * SSD_HEAD_DIM:(h + 1) * SSD_HEAD_DIM] = 1.0
        e[LANES // 2 + offset + h, h * SSD_HEAD_DIM:(h + 1) * SSD_HEAD_DIM] = 1.0
    return jnp.asarray(e, BF16)


def _lane_broadcaster(src):
    e = np.zeros((LANES, SSD_HEADS * LANES), np.float32)
    for h in range(SSD_HEADS):
        for part in range(3):
            e[src + part * SSD_HEADS + h, h * LANES:(h + 1) * LANES] = 1.0
    return jnp.asarray(e, BF16)


def _bstate_kernel(*refs, g_chunks):
    n_in = 5 * g_chunks
    dtb_ref, alog_ref, eb_ref, rb_ref, sb_ref, r_ref, s_ref = refs[n_in:]
    step = pl.program_id(1)

    @pl.when(step == 0)
    def _():
        r_ref[...] = jnp.zeros_like(r_ref)
        s_ref[...] = jnp.zeros_like(s_ref)

    is_last = step == pl.num_programs(1) - 1
    for p in range(g_chunks):
        k_ref, v_ref, xs_ref, b_ref, dt_ref = refs[5 * p:5 * p + 5]
        slot = g_chunks - 1 - p
        _ret_bstate_chunk(k_ref.at[0], v_ref.at[0], rb_ref.at[0, slot], r_ref)
        valid = _valid_rows(is_last if slot == 0 else False)
        _ssd_bstate_chunk(xs_ref.at[0], b_ref.at[0], dt_ref.at[0], dtb_ref, alog_ref, eb_ref,
                          sb_ref.at[0, slot], s_ref, valid)


def _main_kernel(*refs, g_chunks):
    n_in = 8 * g_chunks
    (rb_ref, sb_ref, gnw_ref, dtb_ref, alog_ref, ef_ref, eb_ref, cf_ref, cbk_ref, dskip_ref,
     nw_ref) = refs[n_in:n_in + 11]
    yr_refs = refs[n_in + 11:n_in + 11 + g_chunks]
    ys_refs = refs[n_in + 11 + g_chunks:n_in + 11 + 2 * g_chunks]
    r_ref, s_ref, y_ref = refs[n_in + 11 + 2 * g_chunks:]
    step = pl.program_id(1)

    @pl.when(step == 0)
    def _():
        r_ref[...] = jnp.zeros_like(r_ref)
        s_ref[...] = jnp.zeros_like(s_ref)

    for p in range(g_chunks):
        q_ref, k_ref, v_ref, g_ref, z_ref, xs_ref, bc_ref, dt_ref = refs[8 * p:8 * p + 8]
        _ret_main_chunk(q_ref.at[0], k_ref.at[0], v_ref.at[0], g_ref.at[0], rb_ref.at[0, p], gnw_ref,
                        yr_refs[p].at[0, 0], r_ref)
        valid = _valid_rows(step == 0 if p == 0 else False)
        _ssd_main_chunk(z_ref.at[0], xs_ref.at[0], bc_ref.at[0], dt_ref.at[0], sb_ref.at[0, p], dtb_ref, alog_ref,
                        ef_ref, eb_ref, cf_ref, cbk_ref, dskip_ref, nw_ref, ys_refs[p].at[0, 0], s_ref,
                        y_ref.at[p], valid)


def _chunks_per_step(nc):
    return 3 if nc % 3 == 0 else 1


def _mixers(proj3, dt3, gnw, dtbias, alog, dskip, nw, nb, nc):
    nr = nc - 1
    gc = _chunks_per_step(nc)
    ng = nc // gc
    qb, vb = RET_QK, RET_V
    xw_, bw_, bcw_ = SSD_D_INNER, SSD_BC, 2 * SSD_BC
    e_f = _head_expander(0)
    e_b = _head_expander(SSD_HEADS)
    c_f = _lane_broadcaster(0)
    c_b = _lane_broadcaster(SSD_HEADS)
    params = pltpu.CompilerParams(dimension_semantics=("parallel", "arbitrary"), vmem_limit_bytes=VMEM_LIMIT)

    def const(shape):
        return pl.BlockSpec(shape, lambda b, s: (0,) * len(shape))

    def chunk_spec(width, col, chunk_of):
        return pl.BlockSpec((1, CHUNK, width), lambda b, s: (b, _mem_block(chunk_of(s), nr), col // width))

    bwd_specs, bwd_args = [], []
    for p in range(gc):
        def chunk_of(s, p=p):
            return nc - 1 - (gc * s + p)
        bwd_specs += [chunk_spec(qb, COL_K, chunk_of), chunk_spec(vb, COL_V, chunk_of),
                      chunk_spec(xw_, COL_XS, chunk_of), chunk_spec(bw_, COL_B, chunk_of),
                      pl.BlockSpec((1, CHUNK, DT_W), lambda b, s, f=chunk_of: (b, _mem_block(f(s), nr), 0))]
        bwd_args += [proj3, proj3, proj3, proj3, dt3]

    rb, sb = pl.pallas_call(
        functools.partial(_bstate_kernel, g_chunks=gc),
        grid=(nb, ng),
        in_specs=bwd_specs + [const((1, DT_W)), const((1, DT_W)), const((LANES, SSD_D_INNER))],
        out_specs=[
            pl.BlockSpec((1, gc, RET_QK, RET_V_DIM), lambda b, s: (b, ng - 1 - s, 0, 0)),
            pl.BlockSpec((1, gc, SSD_BC, SSD_GW), lambda b, s: (b, ng - 1 - s, 0, 0)),
        ],
        out_shape=[
            jax.ShapeDtypeStruct((nb, nc, RET_QK, RET_V_DIM), BF16),
            jax.ShapeDtypeStruct((nb, nc, SSD_BC, SSD_GW), BF16),
        ],
        scratch_shapes=[
            pltpu.VMEM((RET_HEADS, RET_QK_DIM, RET_V_DIM), F32),
            pltpu.VMEM((SSD_GROUPS, SSD_STATE, SSD_GW), F32),
        ],
        compiler_params=params,
        name="mix_bwd_state",
    )(*bwd_args, dtbias, alog, e_b)

    fwd_specs, fwd_args = [], []
    for p in range(gc):
        def chunk_of(s, p=p):
            return gc * s + p
        fwd_specs += [chunk_spec(qb, COL_Q, chunk_of), chunk_spec(qb, COL_K, chunk_of),
                      chunk_spec(vb, COL_V, chunk_of), chunk_spec(vb, COL_G, chunk_of),
                      chunk_spec(xw_, COL_Z, chunk_of), chunk_spec(xw_, COL_XS, chunk_of),
                      chunk_spec(bcw_, COL_B, chunk_of),
                      pl.BlockSpec((1, CHUNK, DT_W), lambda b, s, f=chunk_of: (b, _mem_block(f(s), nr), 0))]
        fwd_args += [proj3] * 7 + [dt3]

    outs = pl.pallas_call(
        functools.partial(_main_kernel, g_chunks=gc),
        grid=(nb, ng),
        in_specs=fwd_specs + [
            pl.BlockSpec((1, gc, RET_QK, RET_V_DIM), lambda b, s: (b, s, 0, 0)),
            pl.BlockSpec((1, gc, SSD_BC, SSD_GW), lambda b, s: (b, s, 0, 0)),
            const((1, RET_V)), const((1, DT_W)), const((1, DT_W)),
            const((LANES, SSD_D_INNER)), const((LANES, SSD_D_INNER)),
            const((LANES, SSD_HEADS * LANES)), const((LANES, SSD_HEADS * LANES)),
            const((1, SSD_D_INNER)), const((1, SSD_D_INNER)),
        ],
        out_specs=([pl.BlockSpec((1, 1, CHUNK, RET_V), lambda b, s: (b, s, 0, 0))] * gc
                   + [pl.BlockSpec((1, 1, CHUNK, SSD_D_INNER), lambda b, s: (b, s, 0, 0))] * gc),
        out_shape=([jax.ShapeDtypeStruct((nb, ng, CHUNK, RET_V), BF16)] * gc
                   + [jax.ShapeDtypeStruct((nb, ng, CHUNK, SSD_D_INNER), BF16)] * gc),
        scratch_shapes=[
            pltpu.VMEM((RET_HEADS, RET_QK_DIM, RET_V_DIM), F32),
            pltpu.VMEM((SSD_GROUPS, SSD_STATE, SSD_GW), F32),
            pltpu.VMEM((gc, CHUNK, SSD_D_INNER), F32),
        ],
        compiler_params=params,
        name="mix_main",
    )(*fwd_args, rb, sb, gnw, dtbias, alog, e_f, e_b, c_f, c_b, dskip, nw)
    return list(outs[:gc]), list(outs[gc:])


def _merge_kernel(*refs, g_chunks):
    h_ref = refs[0]
    yr_refs = refs[1:1 + g_chunks]
    ys_refs = refs[1 + g_chunks:1 + 2 * g_chunks]
    gates_ref, wr_ref, ws_ref, wo_ref, o_ref = refs[1 + 2 * g_chunks:]
    yr_in = jnp.concatenate([r[0, 0] for r in yr_refs], axis=0)
    ys_in = jnp.concatenate([r[0, 0] for r in ys_refs], axis=0)
    yr = _dot(yr_in, wr_ref[...])
    ys = _dot(ys_in, ws_ref[...])
    gr = _sigmoid(gates_ref[0, :, :D_MODEL].astype(F32))
    gs = _sigmoid(gates_ref[0, :, D_MODEL:].astype(F32))
    merged = (gr * yr + gs * ys).astype(BF16)
    o_ref[0] = h_ref[0] + _dot(merged, wo_ref[...])


def _merge_call(hm3, yr_list, ys_list, proj3, wr, ws, wo):
    nb, lm, _ = hm3.shape
    gc = len(yr_list)
    ng = yr_list[0].shape[1]
    tm = gc * CHUNK
    gw = 2 * D_MODEL

    def mixer_specs(arrays, width):
        specs, args = [], []
        for q in range(gc):
            if q < gc - 1:
                specs.append(pl.BlockSpec((1, 1, CHUNK, width), lambda b, t: (b, t, 0, 0)))
            else:
                specs.append(pl.BlockSpec((1, 1, CHUNK, width), lambda b, t: (b, lax.rem(t + 1, ng), 0, 0)))
            args.append(arrays[(q + 1) % gc])
        return specs, args

    yr_specs, yr_args = mixer_specs(yr_list, RET_V)
    ys_specs, ys_args = mixer_specs(ys_list, SSD_D_INNER)

    def resident(shape):
        return pl.BlockSpec(shape, lambda b, t: (0,) * len(shape))

    return pl.pallas_call(
        functools.partial(_merge_kernel, g_chunks=gc),
        grid=(nb, lm // tm),
        in_specs=[pl.BlockSpec((1, tm, D_MODEL), lambda b, t: (b, t, 0))] + yr_specs + ys_specs + [
            pl.BlockSpec((1, tm, gw), lambda b, t: (b, t, COL_GATES // gw)),
            resident((RET_V, D_MODEL)), resident((SSD_D_INNER, D_MODEL)), resident((D_MODEL, D_MODEL)),
        ],
        out_specs=pl.BlockSpec((1, tm, D_MODEL), lambda b, t: (b, t, 0)),
        out_shape=jax.ShapeDtypeStruct((nb, lm, D_MODEL), F32),
        compiler_params=pltpu.CompilerParams(
            dimension_semantics=("parallel", "parallel"), vmem_limit_bytes=VMEM_LIMIT),
        name="merge_out",
    )(hm3, *yr_args, *ys_args, proj3, wr, ws, wo)


def _ffn_kernel(h_ref, hp_ref, hn_ref, nw_ref, wup_ref, cw_ref, cb_ref, wd_ref, fnw_ref, o_ref, u_ref, act_ref,
                *, tm, tn):
    nw = nw_ref[...]
    u_ref[0:HALO, :] = _rms(hp_ref[0], nw).astype(BF16)
    u_ref[HALO:HALO + tm, :] = _rms(h_ref[0], nw).astype(BF16)
    u_ref[HALO + tm:2 * HALO + tm, :] = _rms(hn_ref[0], nw).astype(BF16)
    u = u_ref[...]
    nblk = D_FF // tn
    half = (nblk + 1) // 2

    def up(j):
        return (_dot(u, wup_ref[:, j * tn:(j + 1) * tn]),
                _dot(u, wup_ref[:, D_FF + j * tn:D_FF + (j + 1) * tn]))

    nxt = up(0)
    for j in range(nblk):
        fg, fu = nxt
        if j + 1 < nblk:
            nxt = up(j + 1)
        gc = slice(j * tn, (j + 1) * tn)
        uc = slice(D_FF + j * tn, D_FF + (j + 1) * tn)
        act = _silu(_conv_rows(fg, cw_ref, cb_ref, gc, tm)) * _conv_rows(fu, cw_ref, cb_ref, uc, tm)
        act_ref[:, gc] = act.astype(BF16)
        if j == half - 1:
            y_a = _dot(act_ref[:, :half * tn], wd_ref[:half * tn, :])
    y = h_ref[0] + y_a + _dot(act_ref[:, half * tn:], wd_ref[half * tn:, :])
    o_ref[0] = _rms(y, fnw_ref[...])


def _ffn_call(hmid3, nw, w_up, cw, cb, w_down, fnw, nb, s_len):
    lm = hmid3.shape[1]
    tm = 512 if s_len % 512 == 0 else CHUNK
    tn = 256
    nh = lm // HALO
    hpt = tm // HALO

    def prev_blk(t):
        return jnp.where(t == 0, nh - 1, t * hpt - 1)

    def resident(shape):
        return pl.BlockSpec(shape, lambda b, t: (0,) * len(shape), pipeline_mode=pl.Buffered(1))

    return pl.pallas_call(
        functools.partial(_ffn_kernel, tm=tm, tn=tn),
        grid=(nb, s_len // tm),
        in_specs=[
            pl.BlockSpec((1, tm, D_MODEL), lambda b, t: (b, t, 0)),
            pl.BlockSpec((1, HALO, D_MODEL), lambda b, t: (b, prev_blk(t), 0)),
            pl.BlockSpec((1, HALO, D_MODEL), lambda b, t: (b, (t + 1) * hpt, 0)),
            resident((1, D_MODEL)),
            resident((D_MODEL, 2 * D_FF)),
            resident((3, 2 * D_FF)),
            resident((1, 2 * D_FF)),
            resident((D_FF, D_MODEL)),
            resident((1, D_MODEL)),
        ],
        out_specs=pl.BlockSpec((1, tm, D_MODEL), lambda b, t: (b, t, 0)),
        out_shape=jax.ShapeDtypeStruct((nb, s_len, D_MODEL), F32),
        scratch_shapes=[pltpu.VMEM((tm + 2 * HALO, D_MODEL), BF16), pltpu.VMEM((tm, D_FF), BF16)],
        compiler_params=pltpu.CompilerParams(
            dimension_semantics=("parallel", "parallel"), vmem_limit_bytes=VMEM_LIMIT),
        name="ffn",
    )(hmid3, hmid3, hmid3, nw, w_up, cw, cb, w_down, fnw)


def _position_tables(s_len):
    half = RET_QK_DIM // 2
    inv = ROPE_BASE ** (-jnp.arange(half, dtype=F32) / half)
    pos = jnp.concatenate([N_META + jnp.arange(s_len), jnp.zeros((PAD,), jnp.int32), jnp.arange(N_META)])
    ang = pos.astype(F32)[:, None] * inv[None, :]
    cos, sin = jnp.cos(ang), jnp.sin(ang)
    valid = jnp.concatenate([jnp.ones((s_len,), F32), jnp.zeros((PAD,), F32), jnp.ones((N_META,), F32)])
    return (jnp.concatenate([cos, cos], axis=1), jnp.concatenate([-sin, sin], axis=1),
            jnp.broadcast_to(valid[:, None], (s_len + CHUNK, LANES)))


def kernel(x, meta_tokens, norm_mix_w, w_in, ret_gn_w, w_ret_out, w_ssd_conv, b_ssd_conv, dt_bias_f, dt_bias_b, a_log_f, a_log_b, d_skip, ssd_norm_w, w_ssd_out, w_out, norm_ffn_w, w_ffn_up, w_ffn_conv, b_ffn_conv, w_ffn_down, final_norm_w):
    assert norm_mix_w.shape[0] == 1, "single-layer block"
    nb, s_len, _ = x.shape
    assert s_len % CHUNK == 0
    lm = s_len + CHUNK
    nc = lm // CHUNK

    tail = jnp.concatenate([jnp.zeros((PAD, D_MODEL), x.dtype), meta_tokens.astype(x.dtype)], axis=0)
    hm = jnp.concatenate([x, jnp.broadcast_to(tail[None], (nb, CHUNK, D_MODEL))], axis=1)

    wi = w_in[0]
    offs = np.cumsum([0, RET_QK, RET_QK, RET_V, RET_V, SSD_D_INNER, SSD_XBC, SSD_HEADS, SSD_HEADS, D_MODEL, D_MODEL])
    seg = [wi[:, offs[i]:offs[i + 1]] for i in range(10)]
    w_main = jnp.concatenate([seg[4], seg[5], seg[0], seg[1], seg[2], seg[3], seg[8], seg[9]], axis=1).astype(BF16)
    w_dt = jnp.concatenate([seg[6], seg[7], jnp.zeros((D_MODEL, DT_W - 2 * SSD_HEADS), wi.dtype)], axis=1).astype(BF16)
    cos, sin, valid = _position_tables(s_len)

    proj3, dt3 = _proj_call(hm, norm_mix_w, w_main, w_dt, cos, sin, valid, w_ssd_conv[0], b_ssd_conv)

    lane_pad = jnp.zeros((1, DT_W - 2 * SSD_HEADS), F32)
    dtbias = jnp.concatenate([dt_bias_f, dt_bias_b, lane_pad], axis=1)
    alog = jnp.concatenate([a_log_f, a_log_b, lane_pad], axis=1)
    dskip = jnp.repeat(d_skip, SSD_HEAD_DIM, axis=1)
    y_ret, y_ssd = _mixers(proj3, dt3, ret_gn_w, dtbias, alog, dskip, ssd_norm_w, nb, nc)

    h_mid = _merge_call(hm, y_ret, y_ssd, proj3, w_ret_out[0].astype(BF16), w_ssd_out[0].astype(BF16),
                        w_out[0].astype(BF16))

    return _ffn_call(h_mid, norm_ffn_w, w_ffn_up[0].astype(BF16), w_ffn_conv[0],
                     b_ffn_conv, w_ffn_down[0].astype(BF16), final_norm_w.reshape(1, D_MODEL), nb, s_len)
```

```python
import functools
import math

import jax
import jax.numpy as jnp
import numpy as np
from jax import lax
from jax.experimental import pallas as pl
from jax.experimental.pallas import tpu as pltpu

F32 = jnp.float32
BF16 = jnp.bfloat16

D_MODEL = 1024
N_META = 16
CHUNK = 128
PAD = CHUNK - N_META
RET_HEADS = 4
RET_QK_DIM = 128
RET_V_DIM = 256
RET_QK = RET_HEADS * RET_QK_DIM
RET_V = RET_HEADS * RET_V_DIM
SSD_D_INNER = 2 * D_MODEL
SSD_HEAD_DIM = 64
SSD_HEADS = SSD_D_INNER // SSD_HEAD_DIM
SSD_GROUPS = 4
SSD_HPG = SSD_HEADS // SSD_GROUPS
SSD_STATE = 128
SSD_GW = SSD_HPG * SSD_HEAD_DIM
SSD_BC = SSD_GROUPS * SSD_STATE
SSD_XBC = SSD_D_INNER + 2 * SSD_BC
D_FF = 2816
EPS = 1e-6
ROPE_BASE = 10000.0
LOG_GAMMA = tuple(math.log(1.0 - 2.0 ** (-5.0 - h)) for h in range(RET_HEADS))
LOG2E = math.log2(math.e)

COL_Z = 0
COL_XS = 2048
COL_B = 4096
COL_C = 4608
COL_Q = 5120
COL_K = 5632
COL_V = 6144
COL_G = 7168
COL_GATES = 8192
PROJ_W = 10240
PROJ_TN = 1024
QK_TILE = COL_Q // PROJ_TN
CONV_TILE0 = COL_XS // PROJ_TN
CONV_TILES = SSD_XBC // PROJ_TN
DT_W = 128

HALO = 16
LANES = 128
VMEM_LIMIT = 56 * 1024 * 1024


def _sigmoid(x):
    return 0.5 + 0.5 * jnp.tanh(0.5 * x)


def _silu(x):
    hx = 0.5 * x
    return hx + hx * jnp.tanh(hx)


def _softplus(x):
    return jnp.maximum(x, 0.0) + jnp.log1p(jnp.exp(-jnp.abs(x)))


def _split3(x):
    hi = x.astype(BF16)
    r1 = x - hi.astype(F32)
    mid = r1.astype(BF16)
    lo = (r1 - mid.astype(F32)).astype(BF16)
    return hi, mid, lo


def _dot(a, b):
    return jnp.dot(a, b, preferred_element_type=F32)


def _dot_tn(a, b):
    return lax.dot_general(a, b, (((0,), (0,)), ((), ())), preferred_element_type=F32)


def _dot_nt(a, b):
    return lax.dot_general(a, b, (((1,), (1,)), ((), ())), preferred_element_type=F32)


def _row_tile(rows, cap):
    best = CHUNK
    t = CHUNK
    while t <= min(rows, cap):
        if rows % t == 0:
            best = t
        t += CHUNK
    return best


def _rms(x, w):
    ms = jnp.mean(x * x, axis=-1, keepdims=True)
    return x * lax.rsqrt(ms + EPS) * w


def _conv_rows(f, w_ref, b_ref, cols, tm):
    prev = pltpu.roll(f, 1, 0)[HALO:HALO + tm]
    nxt = pltpu.roll(f, f.shape[0] - 1, 0)[HALO:HALO + tm]
    return (w_ref[0:1, cols] * prev + w_ref[1:2, cols] * f[HALO:HALO + tm] + w_ref[2:3, cols] * nxt
            + b_ref[:, cols])


_W_MAIN_COLS = 2 * RET_QK + 2 * RET_V + SSD_D_INNER + SSD_XBC
_TILE_SRC = ([(0, 2 * RET_QK + 2 * RET_V + PROJ_TN * i) for i in range((SSD_D_INNER + SSD_XBC) // PROJ_TN)]
             + [(0, 0), (0, 2 * RET_QK), (0, 2 * RET_QK + RET_V), (1, 0), (1, D_MODEL)])


def _proj_kernel(*refs, tm, n_tail_x, s_len):
    x_ref = refs[0]
    xc_refs = refs[1:1 + n_tail_x]
    (tail_ref, xp_ref, xn_ref, nw_ref, wa_ref, wg_ref, wdt_ref, cos_ref, sin_ref, valid_ref, cw_ref, cb_ref,
     o_ref, odt_ref, hm_ref, u_ref) = refs[1 + n_tail_x:]
    t = pl.program_id(1)
    is_last = t == pl.num_programs(1) - 1
    nw = nw_ref[...]

    def put(rows, x):
        hm_ref[0, rows, :] = x
        u_ref[HALO + rows.start:HALO + rows.stop, :] = _rms(x, nw).astype(BF16)

    @pl.when(jnp.logical_not(is_last))
    def _():
        put(slice(0, tm), x_ref[0])

    @pl.when(is_last)
    def _():
        for i, xc_ref in enumerate(xc_refs):
            put(slice(i * CHUNK, (i + 1) * CHUNK), xc_ref[0])
        put(slice(tm - CHUNK, tm), tail_ref[...])

    @pl.when(t == 0)
    def _():
        u_ref[0:HALO, :] = _rms(tail_ref[CHUNK - HALO:CHUNK, :], nw).astype(BF16)

    @pl.when(t != 0)
    def _():
        u_ref[0:HALO, :] = _rms(xp_ref[0], nw).astype(BF16)

    next_is_tail = (t + 1) * tm == s_len

    @pl.when(next_is_tail)
    def _():
        u_ref[HALO + tm:2 * HALO + tm, :] = _rms(tail_ref[0:HALO, :], nw).astype(BF16)

    @pl.when(jnp.logical_not(next_is_tail))
    def _():
        u_ref[HALO + tm:2 * HALO + tm, :] = _rms(xn_ref[0], nw).astype(BF16)

    u = u_ref[HALO:HALO + tm, :]
    odt_ref[0] = _dot(u, wdt_ref[...])
    cos = cos_ref[...]
    sin = sin_ref[...]
    valid = valid_ref[...]
    w_refs = (wa_ref, wg_ref)
    for j in range(PROJ_W // PROJ_TN):
        c0 = j * PROJ_TN
        src, s0 = _TILE_SRC[j]
        w = w_refs[src][:, s0:s0 + PROJ_TN]
        if CONV_TILE0 <= j < CONV_TILE0 + CONV_TILES:
            f = _dot(u_ref[...], w)
            x0 = c0 - COL_XS
            act = _silu(_conv_rows(f, cw_ref, cb_ref, slice(x0, x0 + PROJ_TN), tm))
            for k in range(PROJ_TN // LANES):
                o_ref[0, :, c0 + k * LANES:c0 + (k + 1) * LANES] = (
                    act[:, k * LANES:(k + 1) * LANES] * valid).astype(BF16)
        elif j == QK_TILE:
            acc = _dot(u, w)
            for hb in range(2 * RET_HEADS):
                a = acc[:, hb * LANES:(hb + 1) * LANES]
                r = a * cos + pltpu.roll(a, RET_QK_DIM // 2, 1) * sin
                if hb >= RET_HEADS:
                    r = r * (RET_QK_DIM ** -0.5)
                o_ref[0, :, c0 + hb * LANES:c0 + (hb + 1) * LANES] = r.astype(BF16)
        else:
            o_ref[0, :, c0:c0 + PROJ_TN] = _dot(u, w).astype(BF16)


def _proj_call(x, tail, nw, wa, wg, wdt, cos, sin, valid, cw, cb):
    nb, s_len, _ = x.shape
    lm = s_len + CHUNK
    tm = _row_tile(lm, 384)
    nt = lm // tm
    hpt = tm // HALO
    n_tail_x = tm // CHUNK - 1
    n_xh = s_len // HALO

    def resident(shape):
        return pl.BlockSpec(shape, lambda b, t: (0,) * len(shape), pipeline_mode=pl.Buffered(1))

    tail_x_specs = [
        pl.BlockSpec((1, CHUNK, D_MODEL), lambda b, t, i=i: (b, s_len // CHUNK - n_tail_x + i, 0))
        for i in range(n_tail_x)]

    return pl.pallas_call(
        functools.partial(_proj_kernel, tm=tm, n_tail_x=n_tail_x, s_len=s_len),
        grid=(nb, nt),
        in_specs=[pl.BlockSpec((1, tm, D_MODEL), lambda b, t: (b, jnp.minimum(t, nt - 2), 0))] + tail_x_specs + [
            resident((CHUNK, D_MODEL)),
            pl.BlockSpec((1, HALO, D_MODEL), lambda b, t: (b, jnp.maximum(t * hpt - 1, 0), 0)),
            pl.BlockSpec((1, HALO, D_MODEL),
                         lambda b, t: (b, jnp.where(t == nt - 1, 0, jnp.minimum((t + 1) * hpt, n_xh - 1)), 0)),
            resident((1, D_MODEL)),
            resident((D_MODEL, _W_MAIN_COLS)),
            resident((D_MODEL, 2 * D_MODEL)),
            resident((D_MODEL, DT_W)),
            pl.BlockSpec((tm, LANES), lambda b, t: (t, 0)),
            pl.BlockSpec((tm, LANES), lambda b, t: (t, 0)),
            pl.BlockSpec((tm, LANES), lambda b, t: (t, 0)),
            resident((3, SSD_XBC)),
            resident((1, SSD_XBC)),
        ],
        out_specs=[
            pl.BlockSpec((1, tm, PROJ_W), lambda b, t: (b, t, 0)),
            pl.BlockSpec((1, tm, DT_W), lambda b, t: (b, t, 0)),
            pl.BlockSpec((1, tm, D_MODEL), lambda b, t: (b, t, 0)),
        ],
        out_shape=[
            jax.ShapeDtypeStruct((nb, lm, PROJ_W), BF16),
            jax.ShapeDtypeStruct((nb, lm, DT_W), F32),
            jax.ShapeDtypeStruct((nb, lm, D_MODEL), F32),
        ],
        scratch_shapes=[pltpu.VMEM((tm + 2 * HALO, D_MODEL), BF16)],
        compiler_params=pltpu.CompilerParams(
            dimension_semantics=("parallel", "parallel"), vmem_limit_bytes=VMEM_LIMIT),
        name="in_proj",
    )(x, *([x] * n_tail_x), tail, x, x, nw, wa, wg, wdt, cos, sin, valid, cw, cb)


def _chunk_pos():
    return lax.broadcasted_iota(jnp.int32, (CHUNK, 1), 0).astype(F32)


def _ret_bstate_chunk(k_ref, v_ref, o_ref, r_ref):
    pos = _chunk_pos()
    for h in range(RET_HEADS):
        lg = LOG_GAMMA[h]
        r = r_ref[h]
        o_ref[h * RET_QK_DIM:(h + 1) * RET_QK_DIM, :] = r.astype(BF16)
        k = k_ref[:, h * RET_QK_DIM:(h + 1) * RET_QK_DIM].astype(F32)
        kd = (k * jnp.exp(lg * pos)).astype(BF16)
        v = v_ref[:, h * RET_V_DIM:(h + 1) * RET_V_DIM]
        r_ref[h] = math.exp(CHUNK * lg) * r + _dot_tn(kd, v)


def _ret_main_chunk(q_ref, k_ref, v_ref, g_ref, rb_ref, gnw_ref, o_ref, r_ref):
    pos = _chunk_pos()
    li = lax.broadcasted_iota(jnp.int32, (CHUNK, CHUNK), 0)
    si = lax.broadcasted_iota(jnp.int32, (CHUNK, CHUNK), 1)
    dist = jnp.abs(li - si).astype(F32)
    for h in range(RET_HEADS):
        lg = LOG_GAMMA[h]
        q = q_ref[:, h * RET_QK_DIM:(h + 1) * RET_QK_DIM]
        k = k_ref[:, h * RET_QK_DIM:(h + 1) * RET_QK_DIM]
        v = v_ref[:, h * RET_V_DIM:(h + 1) * RET_V_DIM]
        r = r_ref[h]
        s = _dot_nt(q, k) * jnp.exp(lg * dist)
        y = _dot(s.astype(BF16), v)
        y = y + jnp.exp(lg * (pos + 1.0)) * _dot(q, r.astype(BF16))
        rb = rb_ref[h * RET_QK_DIM:(h + 1) * RET_QK_DIM, :]
        y = y + jnp.exp(lg * (CHUNK - pos)) * _dot(q, rb)
        kd = (k.astype(F32) * jnp.exp(lg * (CHUNK - 1.0 - pos))).astype(BF16)
        r_ref[h] = math.exp(CHUNK * lg) * r + _dot_tn(kd, v)
        mu = jnp.mean(y, axis=-1, keepdims=True)
        yc = y - mu
        var = jnp.mean(yc * yc, axis=-1, keepdims=True)
        yn = yc * lax.rsqrt(var + EPS)
        sl = slice(h * RET_V_DIM, (h + 1) * RET_V_DIM)
        g = g_ref[:, sl].astype(F32)
        o_ref[:, sl] = (_silu(g) * (yn * gnw_ref[:, sl])).astype(BF16)


def _mem_block(c, nr):
    return jnp.where(c == 0, nr, c - 1)


def _valid_rows(is_meta_chunk):
    row = lax.broadcasted_iota(jnp.int32, (CHUNK, 1), 0)
    return jnp.where(jnp.logical_and(is_meta_chunk, row < PAD), 0.0, 1.0).astype(F32)


def _ssd_decays(dt_ref, dtb_ref, alog_ref, valid):
    dt = _softplus(dt_ref[...] + dtb_ref[...]) * valid
    da = dt * (-LOG2E * jnp.exp(alog_ref[...]))
    li = lax.broadcasted_iota(jnp.int32, (CHUNK, CHUNK), 0)
    si = lax.broadcasted_iota(jnp.int32, (CHUNK, CHUNK), 1)
    tri = jnp.where(si <= li, 1.0, 0.0).astype(BF16)
    hi, mid, lo = _split3(da)
    acs = _dot(tri, hi) + _dot(tri, mid) + _dot(tri, lo)
    return dt, da, acs


def _pack_hi_lo(x):
    lane = lax.broadcasted_iota(jnp.int32, x.shape, 1)
    hi = x.astype(BF16).astype(F32)
    return jnp.where(lane < LANES // 2, hi, pltpu.roll(x - hi, LANES // 2, 1)).astype(BF16)


def _ssd_bstate_chunk(xs_ref, b_ref, dt_ref, dtb_ref, alog_ref, eb_ref, o_ref, s_ref, valid):
    dt, da, acs = _ssd_decays(dt_ref, dtb_ref, alog_ref, valid)
    stack = jnp.concatenate([dt * jnp.exp2(acs - da), jnp.exp2(acs[CHUNK - HALO:CHUNK, :])], axis=0)
    ex = _dot(_pack_hi_lo(stack), eb_ref[...])
    xw = (xs_ref[...].astype(F32) * ex[0:CHUNK, :]).astype(BF16)
    cdec = ex[CHUNK + HALO - 1:CHUNK + HALO, :]
    bm = b_ref[...]
    for g in range(SSD_GROUPS):
        sl = slice(g * SSD_GW, (g + 1) * SSD_GW)
        s = s_ref[g]
        o_ref[g * SSD_STATE:(g + 1) * SSD_STATE, :] = s.astype(BF16)
        s_ref[g] = cdec[:, sl] * s + _dot_tn(bm[:, g * SSD_STATE:(g + 1) * SSD_STATE], xw[:, sl])


def _pack3(x, src):
    lane = lax.broadcasted_iota(jnp.int32, x.shape, 1)
    hi = x.astype(BF16).astype(F32)
    r1 = x - hi
    mid = r1.astype(BF16).astype(F32)
    lo = r1 - mid
    packed = jnp.where(lane < src + SSD_HEADS, hi,
                       jnp.where(lane < src + 2 * SSD_HEADS, pltpu.roll(mid, SSD_HEADS, 1),
                                 pltpu.roll(lo, 2 * SSD_HEADS, 1)))
    return packed.astype(BF16)


def _ssd_main_chunk(z_ref, xs_ref, bc_ref, dt_ref, sb_ref, dtb_ref, alog_ref, ef_ref, eb_ref, cf_ref, cbk_ref,
                    dskip_ref, nw_ref, o_ref, s_ref, y_ref, valid):
    dt, da, acs = _ssd_decays(dt_ref, dtb_ref, alog_ref, valid)
    ecs = acs - da
    last = acs[CHUNK - 1:CHUNK, :]
    ldt_t = jnp.log2(dt).T
    row_f = acs.T - ldt_t
    row_b = ecs.T + ldt_t
    diag_t = jnp.log2(dt + pltpu.roll(dt, LANES - SSD_HEADS, 1)).T
    xs_b = xs_ref[...]
    bc = bc_ref[...]

    li = lax.broadcasted_iota(jnp.int32, (CHUNK, CHUNK), 0)
    si = lax.broadcasted_iota(jnp.int32, (CHUNK, CHUNK), 1)
    below = si < li
    above = si > li
    lane = lax.broadcasted_iota(jnp.int32, (CHUNK, LANES), 1)
    first_half = lane < SSD_HEAD_DIM
    zero = jnp.zeros((CHUNK, LANES), BF16)

    cb = []
    for g in range(SSD_GROUPS):
        bg = bc[:, g * SSD_STATE:(g + 1) * SSD_STATE]
        cg = bc[:, SSD_BC + g * SSD_STATE:SSD_BC + (g + 1) * SSD_STATE]
        cb.append(_dot_nt(cg, bg))

    col_f = _dot(_pack3(acs, 0), cf_ref[...])
    col_b = _dot(_pack3(ecs, SSD_HEADS), cbk_ref[...])

    def mix(h):
        lb = SSD_HEADS + h
        hs = slice(h * LANES, (h + 1) * LANES)
        causal = col_f[:, hs] - row_f[h:h + 1, :]
        anti = row_b[lb:lb + 1, :] - col_b[:, hs]
        e = jnp.where(below, causal, jnp.where(above, anti, diag_t[h:h + 1, :]))
        return (cb[h // SSD_HPG] * jnp.exp2(e)).astype(BF16)

    for p in range(SSD_HEADS // 2):
        xp = xs_b[:, p * LANES:(p + 1) * LANES]
        rhs = jnp.concatenate([jnp.where(first_half, xp, zero), jnp.where(first_half, zero, xp)], axis=0)
        y_ref[:, p * LANES:(p + 1) * LANES] = _dot(jnp.concatenate([mix(2 * p), mix(2 * p + 1)], axis=1), rhs)

    ea = jnp.exp2(acs)
    exf = _dot(jnp.concatenate([_pack_hi_lo(ea), _pack_hi_lo(dt * jnp.exp2(last - acs))], axis=0), ef_ref[...])
    ea_f = exf[0:CHUNK, :]
    w_f = exf[CHUNK:2 * CHUNK, :]
    ea_b = _dot(_pack_hi_lo(jnp.exp2(last - ecs)), eb_ref[...])
    cdec = ea_f[CHUNK - 1:CHUNK, :]
    xs = xs_b.astype(F32)
    xw = (xs * w_f).astype(BF16)
    for g in range(SSD_GROUPS):
        sl = slice(g * SSD_GW, (g + 1) * SSD_GW)
        bg = bc[:, g * SSD_STATE:(g + 1) * SSD_STATE]
        cg = bc[:, SSD_BC + g * SSD_STATE:SSD_BC + (g + 1) * SSD_STATE]
        s = s_ref[g]
        y_off = ea_f[:, sl] * _dot(cg, s.astype(BF16))
        y_off = y_off + ea_b[:, sl] * _dot(cg, sb_ref[g * SSD_STATE:(g + 1) * SSD_STATE, :])
        s_ref[g] = cdec[:, sl] * s + _dot_tn(bg, xw[:, sl])
        y = y_ref[:, sl] + y_off + xs[:, sl] * dskip_ref[:, sl]
        y = y * _silu(z_ref[:, sl].astype(F32))
        o_ref[:, sl] = _rms(y, nw_ref[:, sl]).astype(BF16)


def _head_expander(offset):
    e = np.zeros((LANES, SSD_D_INNER), np.float32)
    for h in range(SSD_HEADS):
        e[offset + h, h * SSD_HEAD_DIM:(h + 1) * SSD_HEAD_DIM] = 1.0
        e[LANES // 2 + offset + h, h * SSD_HEAD_DIM:(h + 1) * SSD_HEAD_DIM] = 1.0
    return jnp.asarray(e, BF16)


def _lane_broadcaster(src):
    e = np.zeros((LANES, SSD_HEADS * LANES), np.float32)
    for h in range(SSD_HEADS):
        for part in range(3):
            e[src + part * SSD_HEADS + h, h * LANES:(h + 1) * LANES] = 1.0
    return jnp.asarray(e, BF16)


def _bstate_kernel(*refs, g_chunks):
    n_in = 5 * g_chunks
    dtb_ref, alog_ref, eb_ref, rb_ref, sb_ref, r_ref, s_ref = refs[n_in:]
    step = pl.program_id(1)

    @pl.when(step == 0)
    def _():
        r_ref[...] = jnp.zeros_like(r_ref)
        s_ref[...] = jnp.zeros_like(s_ref)

    is_last = step == pl.num_programs(1) - 1
    for p in range(g_chunks):
        k_ref, v_ref, xs_ref, b_ref, dt_ref = refs[5 * p:5 * p + 5]
        slot = g_chunks - 1 - p
        _ret_bstate_chunk(k_ref.at[0], v_ref.at[0], rb_ref.at[0, slot], r_ref)
        valid = _valid_rows(is_last if slot == 0 else False)
        _ssd_bstate_chunk(xs_ref.at[0], b_ref.at[0], dt_ref.at[0], dtb_ref, alog_ref, eb_ref,
                          sb_ref.at[0, slot], s_ref, valid)


def _main_kernel(*refs, g_chunks):
    n_in = 8 * g_chunks
    (rb_ref, sb_ref, gnw_ref, dtb_ref, alog_ref, ef_ref, eb_ref, cf_ref, cbk_ref, dskip_ref,
     nw_ref) = refs[n_in:n_in + 11]
    yr_refs = refs[n_in + 11:n_in + 11 + g_chunks]
    ys_refs = refs[n_in + 11 + g_chunks:n_in + 11 + 2 * g_chunks]
    r_ref, s_ref, y_ref = refs[n_in + 11 + 2 * g_chunks:]
    step = pl.program_id(1)

    @pl.when(step == 0)
    def _():
        r_ref[...] = jnp.zeros_like(r_ref)
        s_ref[...] = jnp.zeros_like(s_ref)

    for p in range(g_chunks):
        q_ref, k_ref, v_ref, g_ref, z_ref, xs_ref, bc_ref, dt_ref = refs[8 * p:8 * p + 8]
        _ret_main_chunk(q_ref.at[0], k_ref.at[0], v_ref.at[0], g_ref.at[0], rb_ref.at[0, p], gnw_ref,
                        yr_refs[p].at[0, 0], r_ref)
        valid = _valid_rows(step == 0 if p == 0 else False)
        _ssd_main_chunk(z_ref.at[0], xs_ref.at[0], bc_ref.at[0], dt_ref.at[0], sb_ref.at[0, p], dtb_ref, alog_ref,
                        ef_ref, eb_ref, cf_ref, cbk_ref, dskip_ref, nw_ref, ys_refs[p].at[0, 0], s_ref,
                        y_ref.at[p], valid)


def _chunks_per_step(nc):
    return 3 if nc % 3 == 0 else 1


def _mixers(proj3, dt3, gnw, dtbias, alog, dskip, nw, nb, nc):
    nr = nc - 1
    gc = _chunks_per_step(nc)
    ng = nc // gc
    qb, vb = RET_QK, RET_V
    xw_, bw_, bcw_ = SSD_D_INNER, SSD_BC, 2 * SSD_BC
    e_f = _head_expander(0)
    e_b = _head_expander(SSD_HEADS)
    c_f = _lane_broadcaster(0)
    c_b = _lane_broadcaster(SSD_HEADS)
    params = pltpu.CompilerParams(dimension_semantics=("parallel", "arbitrary"), vmem_limit_bytes=VMEM_LIMIT)

    def const(shape):
        return pl.BlockSpec(shape, lambda b, s: (0,) * len(shape))

    def chunk_spec(width, col, chunk_of):
        return pl.BlockSpec((1, CHUNK, width), lambda b, s: (b, _mem_block(chunk_of(s), nr), col // width))

    bwd_specs, bwd_args = [], []
    for p in range(gc):
        def chunk_of(s, p=p):
            return nc - 1 - (gc * s + p)
        bwd_specs += [chunk_spec(qb, COL_K, chunk_of), chunk_spec(vb, COL_V, chunk_of),
                      chunk_spec(xw_, COL_XS, chunk_of), chunk_spec(bw_, COL_B, chunk_of),
                      pl.BlockSpec((1, CHUNK, DT_W), lambda b, s, f=chunk_of: (b, _mem_block(f(s), nr), 0))]
        bwd_args += [proj3, proj3, proj3, proj3, dt3]

    rb, sb = pl.pallas_call(
        functools.partial(_bstate_kernel, g_chunks=gc),
        grid=(nb, ng),
        in_specs=bwd_specs + [const((1, DT_W)), const((1, DT_W)), const((LANES, SSD_D_INNER))],
        out_specs=[
            pl.BlockSpec((1, gc, RET_QK, RET_V_DIM), lambda b, s: (b, ng - 1 - s, 0, 0)),
            pl.BlockSpec((1, gc, SSD_BC, SSD_GW), lambda b, s: (b, ng - 1 - s, 0, 0)),
        ],
        out_shape=[
            jax.ShapeDtypeStruct((nb, nc, RET_QK, RET_V_DIM), BF16),
            jax.ShapeDtypeStruct((nb, nc, SSD_BC, SSD_GW), BF16),
        ],
        scratch_shapes=[
            pltpu.VMEM((RET_HEADS, RET_QK_DIM, RET_V_DIM), F32),
            pltpu.VMEM((SSD_GROUPS, SSD_STATE, SSD_GW), F32),
        ],
        compiler_params=params,
        name="mix_bwd_state",
    )(*bwd_args, dtbias, alog, e_b)

    fwd_specs, fwd_args = [], []
    for p in range(gc):
        def chunk_of(s, p=p):
            return gc * s + p
        fwd_specs += [chunk_spec(qb, COL_Q, chunk_of), chunk_spec(qb, COL_K, chunk_of),
                      chunk_spec(vb, COL_V, chunk_of), chunk_spec(vb, COL_G, chunk_of),
                      chunk_spec(xw_, COL_Z, chunk_of), chunk_spec(xw_, COL_XS, chunk_of),
                      chunk_spec(bcw_, COL_B, chunk_of),
                      pl.BlockSpec((1, CHUNK, DT_W), lambda b, s, f=chunk_of: (b, _mem_block(f(s), nr), 0))]
        fwd_args += [proj3] * 7 + [dt3]

    outs = pl.pallas_call(
        functools.partial(_main_kernel, g_chunks=gc),
        grid=(nb, ng),
        in_specs=fwd_specs + [
            pl.BlockSpec((1, gc, RET_QK, RET_V_DIM), lambda b, s: (b, s, 0, 0)),
            pl.BlockSpec((1, gc, SSD_BC, SSD_GW), lambda b, s: (b, s, 0, 0)),
            const((1, RET_V)), const((1, DT_W)), const((1, DT_W)),
            const((LANES, SSD_D_INNER)), const((LANES, SSD_D_INNER)),
            const((LANES, SSD_HEADS * LANES)), const((LANES, SSD_HEADS * LANES)),
            const((1, SSD_D_INNER)), const((1, SSD_D_INNER)),
        ],
        out_specs=([pl.BlockSpec((1, 1, CHUNK, RET_V), lambda b, s: (b, s, 0, 0))] * gc
                   + [pl.BlockSpec((1, 1, CHUNK, SSD_D_INNER), lambda b, s: (b, s, 0, 0))] * gc),
        out_shape=([jax.ShapeDtypeStruct((nb, ng, CHUNK, RET_V), BF16)] * gc
                   + [jax.ShapeDtypeStruct((nb, ng, CHUNK, SSD_D_INNER), BF16)] * gc),
        scratch_shapes=[
            pltpu.VMEM((RET_HEADS, RET_QK_DIM, RET_V_DIM), F32),
            pltpu.VMEM((SSD_GROUPS, SSD_STATE, SSD_GW), F32),
            pltpu.VMEM((gc, CHUNK, SSD_D_INNER), F32),
        ],
        compiler_params=params,
        name="mix_main",
    )(*fwd_args, rb, sb, gnw, dtbias, alog, e_f, e_b, c_f, c_b, dskip, nw)
    return list(outs[:gc]), list(outs[gc:])


def _merge_kernel(*refs, g_chunks):
    h_ref = refs[0]
    yr_refs = refs[1:1 + g_chunks]
    ys_refs = refs[1 + g_chunks:1 + 2 * g_chunks]
    gates_ref, wr_ref, ws_ref, wo_ref, o_ref = refs[1 + 2 * g_chunks:]
    yr_in = jnp.concatenate([r[0, 0] for r in yr_refs], axis=0)
    ys_in = jnp.concatenate([r[0, 0] for r in ys_refs], axis=0)
    yr = _dot(yr_in, wr_ref[...])
    ys = _dot(ys_in, ws_ref[...])
    gr = _sigmoid(gates_ref[0, :, :D_MODEL].astype(F32))
    gs = _sigmoid(gates_ref[0, :, D_MODEL:].astype(F32))
    merged = (gr * yr + gs * ys).astype(BF16)
    o_ref[0] = h_ref[0] + _dot(merged, wo_ref[...])


def _merge_call(hm3, yr_list, ys_list, proj3, wr, ws, wo):
    nb, lm, _ = hm3.shape
    gc = len(yr_list)
    ng = yr_list[0].shape[1]
    tm = gc * CHUNK
    gw = 2 * D_MODEL

    def mixer_specs(arrays, width):
        specs, args = [], []
        for q in range(gc):
            if q < gc - 1:
                specs.append(pl.BlockSpec((1, 1, CHUNK, width), lambda b, t: (b, t, 0, 0)))
            else:
                specs.append(pl.BlockSpec((1, 1, CHUNK, width), lambda b, t: (b, lax.rem(t + 1, ng), 0, 0)))
            args.append(arrays[(q + 1) % gc])
        return specs, args

    yr_specs, yr_args = mixer_specs(yr_list, RET_V)
    ys_specs, ys_args = mixer_specs(ys_list, SSD_D_INNER)

    def resident(shape):
        return pl.BlockSpec(shape, lambda b, t: (0,) * len(shape))

    return pl.pallas_call(
        functools.partial(_merge_kernel, g_chunks=gc),
        grid=(nb, lm // tm),
        in_specs=[pl.BlockSpec((1, tm, D_MODEL), lambda b, t: (b, t, 0))] + yr_specs + ys_specs + [
            pl.BlockSpec((1, tm, gw), lambda b, t: (b, t, COL_GATES // gw)),
            resident((RET_V, D_MODEL)), resident((SSD_D_INNER, D_MODEL)), resident((D_MODEL, D_MODEL)),
        ],
        out_specs=pl.BlockSpec((1, tm, D_MODEL), lambda b, t: (b, t, 0)),
        out_shape=jax.ShapeDtypeStruct((nb, lm, D_MODEL), F32),
        compiler_params=pltpu.CompilerParams(
            dimension_semantics=("parallel", "parallel"), vmem_limit_bytes=VMEM_LIMIT),
        name="merge_out",
    )(hm3, *yr_args, *ys_args, proj3, wr, ws, wo)


def _ffn_kernel(h_ref, hp_ref, hn_ref, nw_ref, wup_ref, cw_ref, cb_ref, wd_ref, fnw_ref, o_ref, u_ref, act_ref,
                *, tm, tn):
    nw = nw_ref[...]
    u_ref[0:HALO, :] = _rms(hp_ref[0], nw).astype(BF16)
    u_ref[HALO:HALO + tm, :] = _rms(h_ref[0], nw).astype(BF16)
    u_ref[HALO + tm:2 * HALO + tm, :] = _rms(hn_ref[0], nw).astype(BF16)
    u = u_ref[...]
    nblk = D_FF // tn
    half = (nblk + 1) // 2

    def up(j):
        return (_dot(u, wup_ref[:, j * tn:(j + 1) * tn]),
                _dot(u, wup_ref[:, D_FF + j * tn:D_FF + (j + 1) * tn]))

    nxt = up(0)
    for j in range(nblk):
        fg, fu = nxt
        if j + 1 < nblk:
            nxt = up(j + 1)
        gc = slice(j * tn, (j + 1) * tn)
        uc = slice(D_FF + j * tn, D_FF + (j + 1) * tn)
        act = _silu(_conv_rows(fg, cw_ref, cb_ref, gc, tm)) * _conv_rows(fu, cw_ref, cb_ref, uc, tm)
        act_ref[:, gc] = act.astype(BF16)
        if j == half - 1:
            y_a = _dot(act_ref[:, :half * tn], wd_ref[:half * tn, :])
    y = h_ref[0] + y_a + _dot(act_ref[:, half * tn:], wd_ref[half * tn:, :])
    o_ref[0] = _rms(y, fnw_ref[...])


def _ffn_call(hmid3, nw, w_up, cw, cb, w_down, fnw, nb, s_len):
    lm = hmid3.shape[1]
    tm = 512 if s_len % 512 == 0 else CHUNK
    tn = 256
    nh = lm // HALO
    hpt = tm // HALO

    def prev_blk(t):
        return jnp.where(t == 0, nh - 1, t * hpt - 1)

    def resident(shape):
        return pl.BlockSpec(shape, lambda b, t: (0,) * len(shape), pipeline_mode=pl.Buffered(1))

    return pl.pallas_call(
        functools.partial(_ffn_kernel, tm=tm, tn=tn),
        grid=(nb, s_len // tm),
        in_specs=[
            pl.BlockSpec((1, tm, D_MODEL), lambda b, t: (b, t, 0)),
            pl.BlockSpec((1, HALO, D_MODEL), lambda b, t: (b, prev_blk(t), 0)),
            pl.BlockSpec((1, HALO, D_MODEL), lambda b, t: (b, (t + 1) * hpt, 0)),
            resident((1, D_MODEL)),
            resident((D_MODEL, 2 * D_FF)),
            resident((3, 2 * D_FF)),
            resident((1, 2 * D_FF)),
            resident((D_FF, D_MODEL)),
            resident((1, D_MODEL)),
        ],
        out_specs=pl.BlockSpec((1, tm, D_MODEL), lambda b, t: (b, t, 0)),
        out_shape=jax.ShapeDtypeStruct((nb, s_len, D_MODEL), F32),
        scratch_shapes=[pltpu.VMEM((tm + 2 * HALO, D_MODEL), BF16), pltpu.VMEM((tm, D_FF), BF16)],
        compiler_params=pltpu.CompilerParams(
            dimension_semantics=("parallel", "parallel"), vmem_limit_bytes=VMEM_LIMIT),
        name="ffn",
    )(hmid3, hmid3, hmid3, nw, w_up, cw, cb, w_down, fnw)


def _position_tables(s_len):
    half = RET_QK_DIM // 2
    inv = ROPE_BASE ** (-jnp.arange(half, dtype=F32) / half)
    pos = jnp.concatenate([N_META + jnp.arange(s_len), jnp.zeros((PAD,), jnp.int32), jnp.arange(N_META)])
    ang = pos.astype(F32)[:, None] * inv[None, :]
    cos, sin = jnp.cos(ang), jnp.sin(ang)
    valid = jnp.concatenate([jnp.ones((s_len,), F32), jnp.zeros((PAD,), F32), jnp.ones((N_META,), F32)])
    return (jnp.concatenate([cos, cos], axis=1), jnp.concatenate([-sin, sin], axis=1),
            jnp.broadcast_to(valid[:, None], (s_len + CHUNK, LANES)))


def kernel(x, meta_tokens, norm_mix_w, w_in, ret_gn_w, w_ret_out, w_ssd_conv, b_ssd_conv, dt_bias_f, dt_bias_b, a_log_f, a_log_b, d_skip, ssd_norm_w, w_ssd_out, w_out, norm_ffn_w, w_ffn_up, w_ffn_conv, b_ffn_conv, w_ffn_down, final_norm_w):
    assert norm_mix_w.shape[0] == 1, "single-layer block"
    nb, s_len, _ = x.shape
    assert s_len % CHUNK == 0
    lm = s_len + CHUNK
    nc = lm // CHUNK

    tail = jnp.concatenate([jnp.zeros((PAD, D_MODEL), x.dtype), meta_tokens.astype(x.dtype)], axis=0)

    wi = w_in[0]
    dt0 = _W_MAIN_COLS
    w_a = wi[:, :dt0].astype(BF16)
    w_g = wi[:, dt0 + 2 * SSD_HEADS:].astype(BF16)
    w_dt = jnp.pad(wi[:, dt0:dt0 + 2 * SSD_HEADS], ((0, 0), (0, DT_W - 2 * SSD_HEADS))).astype(BF16)
    cos, sin, valid = _position_tables(s_len)

    proj3, dt3, hm = _proj_call(x, tail, norm_mix_w, w_a, w_g, w_dt, cos, sin, valid, w_ssd_conv[0], b_ssd_conv)

    lane_pad = jnp.zeros((1, DT_W - 2 * SSD_HEADS), F32)
    dtbias = jnp.concatenate([dt_bias_f, dt_bias_b, lane_pad], axis=1)
    alog = jnp.concatenate([a_log_f, a_log_b, lane_pad], axis=1)
    dskip = jnp.repeat(d_skip, SSD_HEAD_DIM, axis=1)
    y_ret, y_ssd = _mixers(proj3, dt3, ret_gn_w, dtbias, alog, dskip, ssd_norm_w, nb, nc)

    h_mid = _merge_call(hm, y_ret, y_ssd, proj3, w_ret_out[0].astype(BF16), w_ssd_out[0].astype(BF16),
                        w_out[0].astype(BF16))

    return _ffn_call(h_mid, norm_ffn_w, w_ffn_up[0].astype(BF16), w_ffn_conv[0],
                     b_ffn_conv, w_ffn_down[0].astype(BF16), final_norm_w.reshape(1, D_MODEL), nb, s_len)
```

```python
import functools
import math

import jax
import jax.numpy as jnp
import numpy as np
from jax import lax
from jax.experimental import pallas as pl
from jax.experimental.pallas import tpu as pltpu

F32 = jnp.float32
BF16 = jnp.bfloat16

D_MODEL = 1024
N_META = 16
CHUNK = 128
PAD = CHUNK - N_META
RET_HEADS = 4
RET_QK_DIM = 128
RET_V_DIM = 256
RET_QK = RET_HEADS * RET_QK_DIM
RET_V = RET_HEADS * RET_V_DIM
SSD_D_INNER = 2 * D_MODEL
SSD_HEAD_DIM = 64
SSD_HEADS = SSD_D_INNER // SSD_HEAD_DIM
SSD_GROUPS = 4
SSD_HPG = SSD_HEADS // SSD_GROUPS
SSD_STATE = 128
SSD_GW = SSD_HPG * SSD_HEAD_DIM
SSD_BC = SSD_GROUPS * SSD_STATE
SSD_XBC = SSD_D_INNER + 2 * SSD_BC
D_FF = 2816
EPS = 1e-6
ROPE_BASE = 10000.0
LOG_GAMMA = tuple(math.log(1.0 - 2.0 ** (-5.0 - h)) for h in range(RET_HEADS))
LOG2E = math.log2(math.e)

COL_Z = 0
COL_XS = 2048
COL_B = 4096
COL_C = 4608
COL_Q = 5120
COL_K = 5632
COL_V = 6144
COL_G = 7168
COL_GATES = 8192
PROJ_W = 10240
PROJ_TN = 1024
QK_TILE = COL_Q // PROJ_TN
CONV_TILE0 = COL_XS // PROJ_TN
CONV_TILES = SSD_XBC // PROJ_TN
DT_W = 128

HALO = 16
LANES = 128
VMEM_LIMIT = 56 * 1024 * 1024


def _sigmoid(x):
    return 0.5 + 0.5 * jnp.tanh(0.5 * x)


def _silu(x):
    hx = 0.5 * x
    return hx + hx * jnp.tanh(hx)


def _softplus(x):
    return jnp.maximum(x, 0.0) + jnp.log1p(jnp.exp(-jnp.abs(x)))


def _split3(x):
    hi = x.astype(BF16)
    r1 = x - hi.astype(F32)
    mid = r1.astype(BF16)
    lo = (r1 - mid.astype(F32)).astype(BF16)
    return hi, mid, lo


def _dot(a, b):
    return jnp.dot(a, b, preferred_element_type=F32)


def _dot_tn(a, b):
    return lax.dot_general(a, b, (((0,), (0,)), ((), ())), preferred_element_type=F32)


def _dot_nt(a, b):
    return lax.dot_general(a, b, (((1,), (1,)), ((), ())), preferred_element_type=F32)


def _row_tile(rows, cap):
    best = CHUNK
    t = CHUNK
    while t <= min(rows, cap):
        if rows % t == 0:
            best = t
        t += CHUNK
    return best


def _rms(x, w):
    ms = jnp.mean(x * x, axis=-1, keepdims=True)
    return x * lax.rsqrt(ms + EPS) * w


def _conv_rows(f, w_ref, b_ref, cols, tm):
    prev = pltpu.roll(f, 1, 0)[HALO:HALO + tm]
    nxt = pltpu.roll(f, f.shape[0] - 1, 0)[HALO:HALO + tm]
    return (w_ref[0:1, cols] * prev + w_ref[1:2, cols] * f[HALO:HALO + tm] + w_ref[2:3, cols] * nxt
            + b_ref[:, cols])


_W_MAIN_COLS = 2 * RET_QK + 2 * RET_V + SSD_D_INNER + SSD_XBC
_TILE_SRC = ([(0, 2 * RET_QK + 2 * RET_V + PROJ_TN * i) for i in range((SSD_D_INNER + SSD_XBC) // PROJ_TN)]
             + [(0, 0), (0, 2 * RET_QK), (0, 2 * RET_QK + RET_V), (1, 0), (1, D_MODEL)])


def _proj_kernel(*refs, tm, n_tail_x, s_len):
    x_ref = refs[0]
    xc_refs = refs[1:1 + n_tail_x]
    (tail_ref, xp_ref, xn_ref, nw_ref, wa_ref, wg_ref, wdt_ref, cos_ref, sin_ref, valid_ref, cw_ref, cb_ref,
     o_ref, odt_ref, hm_ref, u_ref) = refs[1 + n_tail_x:]
    t = pl.program_id(1)
    is_last = t == pl.num_programs(1) - 1
    nw = nw_ref[...]

    def put(rows, x):
        hm_ref[0, rows, :] = x
        u_ref[HALO + rows.start:HALO + rows.stop, :] = _rms(x, nw).astype(BF16)

    @pl.when(jnp.logical_not(is_last))
    def _():
        put(slice(0, tm), x_ref[0])

    @pl.when(is_last)
    def _():
        for i, xc_ref in enumerate(xc_refs):
            put(slice(i * CHUNK, (i + 1) * CHUNK), xc_ref[0])
        put(slice(tm - CHUNK, tm), tail_ref[...])

    @pl.when(t == 0)
    def _():
        u_ref[0:HALO, :] = _rms(tail_ref[CHUNK - HALO:CHUNK, :], nw).astype(BF16)

    @pl.when(t != 0)
    def _():
        u_ref[0:HALO, :] = _rms(xp_ref[0], nw).astype(BF16)

    next_is_tail = (t + 1) * tm == s_len

    @pl.when(next_is_tail)
    def _():
        u_ref[HALO + tm:2 * HALO + tm, :] = _rms(tail_ref[0:HALO, :], nw).astype(BF16)

    @pl.when(jnp.logical_not(next_is_tail))
    def _():
        u_ref[HALO + tm:2 * HALO + tm, :] = _rms(xn_ref[0], nw).astype(BF16)

    u = u_ref[HALO:HALO + tm, :]
    odt_ref[0] = _dot(u, wdt_ref[...])
    cos = cos_ref[...]
    sin = sin_ref[...]
    valid = valid_ref[...]
    w_refs = (wa_ref, wg_ref)
    for j in range(PROJ_W // PROJ_TN):
        c0 = j * PROJ_TN
        src, s0 = _TILE_SRC[j]
        w = w_refs[src][:, s0:s0 + PROJ_TN]
        if CONV_TILE0 <= j < CONV_TILE0 + CONV_TILES:
            f = _dot(u_ref[...], w)
            x0 = c0 - COL_XS
            act = _silu(_conv_rows(f, cw_ref, cb_ref, slice(x0, x0 + PROJ_TN), tm))
            for k in range(PROJ_TN // LANES):
                o_ref[0, :, c0 + k * LANES:c0 + (k + 1) * LANES] = (
                    act[:, k * LANES:(k + 1) * LANES] * valid).astype(BF16)
        elif j == QK_TILE:
            acc = _dot(u, w)
            for hb in range(2 * RET_HEADS):
                a = acc[:, hb * LANES:(hb + 1) * LANES]
                r = a * cos + pltpu.roll(a, RET_QK_DIM // 2, 1) * sin
                if hb >= RET_HEADS:
                    r = r * (RET_QK_DIM ** -0.5)
                o_ref[0, :, c0 + hb * LANES:c0 + (hb + 1) * LANES] = r.astype(BF16)
        else:
            o_ref[0, :, c0:c0 + PROJ_TN] = _dot(u, w).astype(BF16)


def _proj_call(x, tail, nw, wa, wg, wdt, cos, sin, valid, cw, cb):
    nb, s_len, _ = x.shape
    lm = s_len + CHUNK
    tm = _row_tile(lm, 384)
    nt = lm // tm
    hpt = tm // HALO
    n_tail_x = tm // CHUNK - 1
    n_xh = s_len // HALO

    def resident(shape):
        return pl.BlockSpec(shape, lambda b, t: (0,) * len(shape), pipeline_mode=pl.Buffered(1))

    tail_x_specs = [
        pl.BlockSpec((1, CHUNK, D_MODEL), lambda b, t, i=i: (b, s_len // CHUNK - n_tail_x + i, 0))
        for i in range(n_tail_x)]

    return pl.pallas_call(
        functools.partial(_proj_kernel, tm=tm, n_tail_x=n_tail_x, s_len=s_len),
        grid=(nb, nt),
        in_specs=[pl.BlockSpec((1, tm, D_MODEL), lambda b, t: (b, jnp.minimum(t, nt - 2), 0))] + tail_x_specs + [
            resident((CHUNK, D_MODEL)),
            pl.BlockSpec((1, HALO, D_MODEL), lambda b, t: (b, jnp.maximum(t * hpt - 1, 0), 0)),
            pl.BlockSpec((1, HALO, D_MODEL),
                         lambda b, t: (b, jnp.where(t == nt - 1, 0, jnp.minimum((t + 1) * hpt, n_xh - 1)), 0)),
            resident((1, D_MODEL)),
            resident((D_MODEL, _W_MAIN_COLS)),
            resident((D_MODEL, 2 * D_MODEL)),
            resident((D_MODEL, DT_W)),
            pl.BlockSpec((tm, LANES), lambda b, t: (t, 0)),
            pl.BlockSpec((tm, LANES), lambda b, t: (t, 0)),
            pl.BlockSpec((tm, LANES), lambda b, t: (t, 0)),
            resident((3, SSD_XBC)),
            resident((1, SSD_XBC)),
        ],
        out_specs=[
            pl.BlockSpec((1, tm, PROJ_W), lambda b, t: (b, t, 0)),
            pl.BlockSpec((1, tm, DT_W), lambda b, t: (b, t, 0)),
            pl.BlockSpec((1, tm, D_MODEL), lambda b, t: (b, t, 0)),
        ],
        out_shape=[
            jax.ShapeDtypeStruct((nb, lm, PROJ_W), BF16),
            jax.ShapeDtypeStruct((nb, lm, DT_W), F32),
            jax.ShapeDtypeStruct((nb, lm, D_MODEL), F32),
        ],
        scratch_shapes=[pltpu.VMEM((tm + 2 * HALO, D_MODEL), BF16)],
        compiler_params=pltpu.CompilerParams(
            dimension_semantics=("parallel", "parallel"), vmem_limit_bytes=VMEM_LIMIT),
        name="in_proj",
    )(x, *([x] * n_tail_x), tail, x, x, nw, wa, wg, wdt, cos, sin, valid, cw, cb)


def _chunk_pos():
    return lax.broadcasted_iota(jnp.int32, (CHUNK, 1), 0).astype(F32)


def _ret_bstate_chunk(k_ref, v_ref, o_ref, r_ref):
    pos = _chunk_pos()
    for h in range(RET_HEADS):
        lg = LOG_GAMMA[h]
        r = r_ref[h]
        o_ref[h * RET_QK_DIM:(h + 1) * RET_QK_DIM, :] = r.astype(BF16)
        k = k_ref[:, h * RET_QK_DIM:(h + 1) * RET_QK_DIM].astype(F32)
        kd = (k * jnp.exp(lg * pos)).astype(BF16)
        v = v_ref[:, h * RET_V_DIM:(h + 1) * RET_V_DIM]
        r_ref[h] = math.exp(CHUNK * lg) * r + _dot_tn(kd, v)


def _ret_main_chunk(q_ref, k_ref, v_ref, g_ref, rb_ref, gnw_ref, o_ref, r_ref):
    pos = _chunk_pos()
    li = lax.broadcasted_iota(jnp.int32, (CHUNK, CHUNK), 0)
    si = lax.broadcasted_iota(jnp.int32, (CHUNK, CHUNK), 1)
    dist = jnp.abs(li - si).astype(F32)
    for h in range(RET_HEADS):
        lg = LOG_GAMMA[h]
        q = q_ref[:, h * RET_QK_DIM:(h + 1) * RET_QK_DIM]
        k = k_ref[:, h * RET_QK_DIM:(h + 1) * RET_QK_DIM]
        v = v_ref[:, h * RET_V_DIM:(h + 1) * RET_V_DIM]
        r = r_ref[h]
        s = _dot_nt(q, k) * jnp.exp(lg * dist)
        y = _dot(s.astype(BF16), v)
        y = y + jnp.exp(lg * (pos + 1.0)) * _dot(q, r.astype(BF16))
        rb = rb_ref[h * RET_QK_DIM:(h + 1) * RET_QK_DIM, :]
        y = y + jnp.exp(lg * (CHUNK - pos)) * _dot(q, rb)
        kd = (k.astype(F32) * jnp.exp(lg * (CHUNK - 1.0 - pos))).astype(BF16)
        r_ref[h] = math.exp(CHUNK * lg) * r + _dot_tn(kd, v)
        mu = jnp.mean(y, axis=-1, keepdims=True)
        yc = y - mu
        var = jnp.mean(yc * yc, axis=-1, keepdims=True)
        yn = yc * lax.rsqrt(var + EPS)
        sl = slice(h * RET_V_DIM, (h + 1) * RET_V_DIM)
        g = g_ref[:, sl].astype(F32)
        o_ref[:, sl] = (_silu(g) * (yn * gnw_ref[:, sl])).astype(BF16)


def _mem_block(c, nr):
    return jnp.where(c == 0, nr, c - 1)


def _valid_rows(is_meta_chunk):
    row = lax.broadcasted_iota(jnp.int32, (CHUNK, 1), 0)
    return jnp.where(jnp.logical_and(is_meta_chunk, row < PAD), 0.0, 1.0).astype(F32)


def _ssd_decays(dt_ref, dtb_ref, alog_ref, valid):
    dt = _softplus(dt_ref[...] + dtb_ref[...]) * valid
    da = dt * (-LOG2E * jnp.exp(alog_ref[...]))
    li = lax.broadcasted_iota(jnp.int32, (CHUNK, CHUNK), 0)
    si = lax.broadcasted_iota(jnp.int32, (CHUNK, CHUNK), 1)
    tri = jnp.where(si <= li, 1.0, 0.0).astype(BF16)
    hi, mid, lo = _split3(da)
    acs = _dot(tri, hi) + _dot(tri, mid) + _dot(tri, lo)
    return dt, da, acs


def _pack_hi_lo(x):
    lane = lax.broadcasted_iota(jnp.int32, x.shape, 1)
    hi = x.astype(BF16).astype(F32)
    return jnp.where(lane < LANES // 2, hi, pltpu.roll(x - hi, LANES // 2, 1)).astype(BF16)


def _ssd_bstate_chunk(xs_ref, b_ref, dt_ref, dtb_ref, alog_ref, eb_ref, o_ref, s_ref, valid):
    dt, da, acs = _ssd_decays(dt_ref, dtb_ref, alog_ref, valid)
    stack = jnp.concatenate([dt * jnp.exp2(acs - da), jnp.exp2(acs[CHUNK - HALO:CHUNK, :])], axis=0)
    ex = _dot(_pack_hi_lo(stack), eb_ref[...])
    xw = (xs_ref[...].astype(F32) * ex[0:CHUNK, :]).astype(BF16)
    cdec = ex[CHUNK + HALO - 1:CHUNK + HALO, :]
    bm = b_ref[...]
    for g in range(SSD_GROUPS):
        sl = slice(g * SSD_GW, (g + 1) * SSD_GW)
        s = s_ref[g]
        o_ref[g * SSD_STATE:(g + 1) * SSD_STATE, :] = s.astype(BF16)
        s_ref[g] = cdec[:, sl] * s + _dot_tn(bm[:, g * SSD_STATE:(g + 1) * SSD_STATE], xw[:, sl])


def _pack3(x, src):
    lane = lax.broadcasted_iota(jnp.int32, x.shape, 1)
    hi = x.astype(BF16).astype(F32)
    r1 = x - hi
    mid = r1.astype(BF16).astype(F32)
    lo = r1 - mid
    packed = jnp.where(lane < src + SSD_HEADS, hi,
                       jnp.where(lane < src + 2 * SSD_HEADS, pltpu.roll(mid, SSD_HEADS, 1),
                                 pltpu.roll(lo, 2 * SSD_HEADS, 1)))
    return packed.astype(BF16)


def _ssd_main_chunk(z_ref, xs_ref, bc_ref, dt_ref, sb_ref, dtb_ref, alog_ref, ef_ref, cf_ref,
                    dskip_ref, nw_ref, o_ref, s_ref, y_ref, valid):
    dt, da, acs = _ssd_decays(dt_ref, dtb_ref, alog_ref, valid)
    last = acs[CHUNK - 1:CHUNK, :]
    to_right = last - (acs - da)
    ldt_t = jnp.log2(dt).T
    row_f = acs.T - ldt_t
    row_b = ldt_t - to_right.T
    diag_t = jnp.log2(dt + pltpu.roll(dt, LANES - SSD_HEADS, 1)).T
    xs_b = xs_ref[...]
    bc = bc_ref[...]

    li = lax.broadcasted_iota(jnp.int32, (CHUNK, CHUNK), 0)
    si = lax.broadcasted_iota(jnp.int32, (CHUNK, CHUNK), 1)
    below = si < li
    above = si > li
    lane = lax.broadcasted_iota(jnp.int32, (CHUNK, LANES), 1)
    first_half = lane < SSD_HEAD_DIM
    zero = jnp.zeros((CHUNK, LANES), BF16)

    cb = []
    for g in range(SSD_GROUPS):
        bg = bc[:, g * SSD_STATE:(g + 1) * SSD_STATE]
        cg = bc[:, SSD_BC + g * SSD_STATE:SSD_BC + (g + 1) * SSD_STATE]
        cb.append(_dot_nt(cg, bg))

    col_f = _dot(_pack3(acs, 0), cf_ref[...])

    def mix(h, causal_col, anti_col):
        lb = SSD_HEADS + h
        causal = causal_col - row_f[h:h + 1, :]
        anti = row_b[lb:lb + 1, :] + anti_col
        e = jnp.where(below, causal, jnp.where(above, anti, diag_t[h:h + 1, :]))
        return (cb[h // SSD_HPG] * jnp.exp2(e)).astype(BF16)

    ea_f, ea_b = [], []
    for p in range(SSD_HEADS // 2):
        ha, hb = 2 * p, 2 * p + 1
        fa = col_f[:, ha * LANES:(ha + 1) * LANES]
        fb = col_f[:, hb * LANES:(hb + 1) * LANES]
        ra = jnp.broadcast_to(to_right[:, SSD_HEADS + ha:SSD_HEADS + ha + 1], (CHUNK, LANES))
        rb = jnp.broadcast_to(to_right[:, SSD_HEADS + hb:SSD_HEADS + hb + 1], (CHUNK, LANES))
        ea_f.append(jnp.exp2(jnp.where(first_half, fa, fb)))
        ea_b.append(jnp.exp2(jnp.where(first_half, ra, rb)))
        xp = xs_b[:, p * LANES:(p + 1) * LANES]
        rhs = jnp.concatenate([jnp.where(first_half, xp, zero), jnp.where(first_half, zero, xp)], axis=0)
        y_ref[:, p * LANES:(p + 1) * LANES] = _dot(jnp.concatenate([mix(ha, fa, ra), mix(hb, fb, rb)], axis=1), rhs)

    w_f = _dot(_pack_hi_lo(dt * jnp.exp2(last - acs)), ef_ref[...])
    xs = xs_b.astype(F32)
    xw = (xs * w_f).astype(BF16)
    tiles = SSD_GW // LANES
    for g in range(SSD_GROUPS):
        sl = slice(g * SSD_GW, (g + 1) * SSD_GW)
        bg = bc[:, g * SSD_STATE:(g + 1) * SSD_STATE]
        cg = bc[:, SSD_BC + g * SSD_STATE:SSD_BC + (g + 1) * SSD_STATE]
        ea_fg = jnp.concatenate(ea_f[g * tiles:(g + 1) * tiles], axis=1)
        ea_bg = jnp.concatenate(ea_b[g * tiles:(g + 1) * tiles], axis=1)
        s = s_ref[g]
        y_off = ea_fg * _dot(cg, s.astype(BF16))
        y_off = y_off + ea_bg * _dot(cg, sb_ref[g * SSD_STATE:(g + 1) * SSD_STATE, :])
        s_ref[g] = ea_fg[CHUNK - 1:CHUNK, :] * s + _dot_tn(bg, xw[:, sl])
        y = y_ref[:, sl] + y_off + xs[:, sl] * dskip_ref[:, sl]
        y = y * _silu(z_ref[:, sl].astype(F32))
        o_ref[:, sl] = _rms(y, nw_ref[:, sl]).astype(BF16)


def _head_expander(offset):
    e = np.zeros((LANES, SSD_D_INNER), np.float32)
    for h in range(SSD_HEADS):
        e[offset + h, h * SSD_HEAD_DIM:(h + 1) * SSD_HEAD_DIM] = 1.0
        e[LANES // 2 + offset + h, h * SSD_HEAD_DIM:(h + 1) * SSD_HEAD_DIM] = 1.0
    return jnp.asarray(e, BF16)


def _lane_broadcaster(src):
    e = np.zeros((LANES, SSD_HEADS * LANES), np.float32)
    for h in range(SSD_HEADS):
        for part in range(3):
            e[src + part * SSD_HEADS + h, h * LANES:(h + 1) * LANES] = 1.0
    return jnp.asarray(e, BF16)


def _bstate_kernel(*refs, g_chunks):
    n_in = 5 * g_chunks
    dtb_ref, alog_ref, eb_ref, rb_ref, sb_ref, r_ref, s_ref = refs[n_in:]
    step = pl.program_id(1)

    @pl.when(step == 0)
    def _():
        r_ref[...] = jnp.zeros_like(r_ref)
        s_ref[...] = jnp.zeros_like(s_ref)

    is_last = step == pl.num_programs(1) - 1
    for p in range(g_chunks):
        k_ref, v_ref, xs_ref, b_ref, dt_ref = refs[5 * p:5 * p + 5]
        slot = g_chunks - 1 - p
        _ret_bstate_chunk(k_ref.at[0], v_ref.at[0], rb_ref.at[0, slot], r_ref)
        valid = _valid_rows(is_last if slot == 0 else False)
        _ssd_bstate_chunk(xs_ref.at[0], b_ref.at[0], dt_ref.at[0], dtb_ref, alog_ref, eb_ref,
                          sb_ref.at[0, slot], s_ref, valid)


def _main_kernel(*refs, g_chunks):
    n_in = 8 * g_chunks
    rb_ref, sb_ref, gnw_ref, dtb_ref, alog_ref, ef_ref, cf_ref, dskip_ref, nw_ref = refs[n_in:n_in + 9]
    yr_refs = refs[n_in + 9:n_in + 9 + g_chunks]
    ys_refs = refs[n_in + 9 + g_chunks:n_in + 9 + 2 * g_chunks]
    r_ref, s_ref, y_ref = refs[n_in + 9 + 2 * g_chunks:]
    step = pl.program_id(1)

    @pl.when(step == 0)
    def _():
        r_ref[...] = jnp.zeros_like(r_ref)
        s_ref[...] = jnp.zeros_like(s_ref)

    for p in range(g_chunks):
        q_ref, k_ref, v_ref, g_ref, z_ref, xs_ref, bc_ref, dt_ref = refs[8 * p:8 * p + 8]
        _ret_main_chunk(q_ref.at[0], k_ref.at[0], v_ref.at[0], g_ref.at[0], rb_ref.at[0, p], gnw_ref,
                        yr_refs[p].at[0, 0], r_ref)
        valid = _valid_rows(step == 0 if p == 0 else False)
        _ssd_main_chunk(z_ref.at[0], xs_ref.at[0], bc_ref.at[0], dt_ref.at[0], sb_ref.at[0, p], dtb_ref, alog_ref,
                        ef_ref, cf_ref, dskip_ref, nw_ref, ys_refs[p].at[0, 0], s_ref,
                        y_ref.at[p], valid)


def _chunks_per_step(nc):
    return 3 if nc % 3 == 0 else 1


def _mixers(proj3, dt3, gnw, dtbias, alog, dskip, nw, nb, nc):
    nr = nc - 1
    gc = _chunks_per_step(nc)
    ng = nc // gc
    qb, vb = RET_QK, RET_V
    xw_, bw_, bcw_ = SSD_D_INNER, SSD_BC, 2 * SSD_BC
    e_f = _head_expander(0)
    e_b = _head_expander(SSD_HEADS)
    c_f = _lane_broadcaster(0)
    params = pltpu.CompilerParams(dimension_semantics=("parallel", "arbitrary"), vmem_limit_bytes=VMEM_LIMIT)

    def const(shape):
        return pl.BlockSpec(shape, lambda b, s: (0,) * len(shape))

    def chunk_spec(width, col, chunk_of):
        return pl.BlockSpec((1, CHUNK, width), lambda b, s: (b, _mem_block(chunk_of(s), nr), col // width))

    bwd_specs, bwd_args = [], []
    for p in range(gc):
        def chunk_of(s, p=p):
            return nc - 1 - (gc * s + p)
        bwd_specs += [chunk_spec(qb, COL_K, chunk_of), chunk_spec(vb, COL_V, chunk_of),
                      chunk_spec(xw_, COL_XS, chunk_of), chunk_spec(bw_, COL_B, chunk_of),
                      pl.BlockSpec((1, CHUNK, DT_W), lambda b, s, f=chunk_of: (b, _mem_block(f(s), nr), 0))]
        bwd_args += [proj3, proj3, proj3, proj3, dt3]

    rb, sb = pl.pallas_call(
        functools.partial(_bstate_kernel, g_chunks=gc),
        grid=(nb, ng),
        in_specs=bwd_specs + [const((1, DT_W)), const((1, DT_W)), const((LANES, SSD_D_INNER))],
        out_specs=[
            pl.BlockSpec((1, gc, RET_QK, RET_V_DIM), lambda b, s: (b, ng - 1 - s, 0, 0)),
            pl.BlockSpec((1, gc, SSD_BC, SSD_GW), lambda b, s: (b, ng - 1 - s, 0, 0)),
        ],
        out_shape=[
            jax.ShapeDtypeStruct((nb, nc, RET_QK, RET_V_DIM), BF16),
            jax.ShapeDtypeStruct((nb, nc, SSD_BC, SSD_GW), BF16),
        ],
        scratch_shapes=[
            pltpu.VMEM((RET_HEADS, RET_QK_DIM, RET_V_DIM), F32),
            pltpu.VMEM((SSD_GROUPS, SSD_STATE, SSD_GW), F32),
        ],
        compiler_params=params,
        name="mix_bwd_state",
    )(*bwd_args, dtbias, alog, e_b)

    fwd_specs, fwd_args = [], []
    for p in range(gc):
        def chunk_of(s, p=p):
            return gc * s + p
        fwd_specs += [chunk_spec(qb, COL_Q, chunk_of), chunk_spec(qb, COL_K, chunk_of),
                      chunk_spec(vb, COL_V, chunk_of), chunk_spec(vb, COL_G, chunk_of),
                      chunk_spec(xw_, COL_Z, chunk_of), chunk_spec(xw_, COL_XS, chunk_of),
                      chunk_spec(bcw_, COL_B, chunk_of),
                      pl.BlockSpec((1, CHUNK, DT_W), lambda b, s, f=chunk_of: (b, _mem_block(f(s), nr), 0))]
        fwd_args += [proj3] * 7 + [dt3]

    outs = pl.pallas_call(
        functools.partial(_main_kernel, g_chunks=gc),
        grid=(nb, ng),
        in_specs=fwd_specs + [
            pl.BlockSpec((1, gc, RET_QK, RET_V_DIM), lambda b, s: (b, s, 0, 0)),
            pl.BlockSpec((1, gc, SSD_BC, SSD_GW), lambda b, s: (b, s, 0, 0)),
            const((1, RET_V)), const((1, DT_W)), const((1, DT_W)),
            const((LANES, SSD_D_INNER)), const((LANES, SSD_HEADS * LANES)),
            const((1, SSD_D_INNER)), const((1, SSD_D_INNER)),
        ],
        out_specs=([pl.BlockSpec((1, 1, CHUNK, RET_V), lambda b, s: (b, s, 0, 0))] * gc
                   + [pl.BlockSpec((1, 1, CHUNK, SSD_D_INNER), lambda b, s: (b, s, 0, 0))] * gc),
        out_shape=([jax.ShapeDtypeStruct((nb, ng, CHUNK, RET_V), BF16)] * gc
                   + [jax.ShapeDtypeStruct((nb, ng, CHUNK, SSD_D_INNER), BF16)] * gc),
        scratch_shapes=[
            pltpu.VMEM((RET_HEADS, RET_QK_DIM, RET_V_DIM), F32),
            pltpu.VMEM((SSD_GROUPS, SSD_STATE, SSD_GW), F32),
            pltpu.VMEM((gc, CHUNK, SSD_D_INNER), F32),
        ],
        compiler_params=params,
        name="mix_main",
    )(*fwd_args, rb, sb, gnw, dtbias, alog, e_f, c_f, dskip, nw)
    return list(outs[:gc]), list(outs[gc:])


def _merge_kernel(*refs, g_chunks):
    h_ref = refs[0]
    yr_refs = refs[1:1 + g_chunks]
    ys_refs = refs[1 + g_chunks:1 + 2 * g_chunks]
    gates_ref, wr_ref, ws_ref, wo_ref, o_ref = refs[1 + 2 * g_chunks:]
    yr_in = jnp.concatenate([r[0, 0] for r in yr_refs], axis=0)
    ys_in = jnp.concatenate([r[0, 0] for r in ys_refs], axis=0)
    yr = _dot(yr_in, wr_ref[...])
    ys = _dot(ys_in, ws_ref[...])
    gr = _sigmoid(gates_ref[0, :, :D_MODEL].astype(F32))
    gs = _sigmoid(gates_ref[0, :, D_MODEL:].astype(F32))
    merged = (gr * yr + gs * ys).astype(BF16)
    o_ref[0] = h_ref[0] + _dot(merged, wo_ref[...])


def _merge_call(hm3, yr_list, ys_list, proj3, wr, ws, wo):
    nb, lm, _ = hm3.shape
    gc = len(yr_list)
    ng = yr_list[0].shape[1]
    tm = gc * CHUNK
    gw = 2 * D_MODEL

    def mixer_specs(arrays, width):
        specs, args = [], []
        for q in range(gc):
            if q < gc - 1:
                specs.append(pl.BlockSpec((1, 1, CHUNK, width), lambda b, t: (b, t, 0, 0)))
            else:
                specs.append(pl.BlockSpec((1, 1, CHUNK, width), lambda b, t: (b, lax.rem(t + 1, ng), 0, 0)))
            args.append(arrays[(q + 1) % gc])
        return specs, args

    yr_specs, yr_args = mixer_specs(yr_list, RET_V)
    ys_specs, ys_args = mixer_specs(ys_list, SSD_D_INNER)

    def resident(shape):
        return pl.BlockSpec(shape, lambda b, t: (0,) * len(shape))

    return pl.pallas_call(
        functools.partial(_merge_kernel, g_chunks=gc),
        grid=(nb, lm // tm),
        in_specs=[pl.BlockSpec((1, tm, D_MODEL), lambda b, t: (b, t, 0))] + yr_specs + ys_specs + [
            pl.BlockSpec((1, tm, gw), lambda b, t: (b, t, COL_GATES // gw)),
            resident((RET_V, D_MODEL)), resident((SSD_D_INNER, D_MODEL)), resident((D_MODEL, D_MODEL)),
        ],
        out_specs=pl.BlockSpec((1, tm, D_MODEL), lambda b, t: (b, t, 0)),
        out_shape=jax.ShapeDtypeStruct((nb, lm, D_MODEL), F32),
        compiler_params=pltpu.CompilerParams(
            dimension_semantics=("parallel", "parallel"), vmem_limit_bytes=VMEM_LIMIT),
        name="merge_out",
    )(hm3, *yr_args, *ys_args, proj3, wr, ws, wo)


def _ffn_kernel(h_ref, hp_ref, hn_ref, nw_ref, wup_ref, cw_ref, cb_ref, wd_ref, fnw_ref, o_ref, u_ref, act_ref,
                *, tm, tn):
    nw = nw_ref[...]
    u_ref[0:HALO, :] = _rms(hp_ref[0], nw).astype(BF16)
    u_ref[HALO:HALO + tm, :] = _rms(h_ref[0], nw).astype(BF16)
    u_ref[HALO + tm:2 * HALO + tm, :] = _rms(hn_ref[0], nw).astype(BF16)
    u = u_ref[...]
    nblk = D_FF // tn
    half = (nblk + 1) // 2

    def up(j):
        return (_dot(u, wup_ref[:, j * tn:(j + 1) * tn]),
                _dot(u, wup_ref[:, D_FF + j * tn:D_FF + (j + 1) * tn]))

    nxt = up(0)
    for j in range(nblk):
        fg, fu = nxt
        if j + 1 < nblk:
            nxt = up(j + 1)
        gc = slice(j * tn, (j + 1) * tn)
        uc = slice(D_FF + j * tn, D_FF + (j + 1) * tn)
        act = _silu(_conv_rows(fg, cw_ref, cb_ref, gc, tm)) * _conv_rows(fu, cw_ref, cb_ref, uc, tm)
        act_ref[:, gc] = act.astype(BF16)
        if j == half - 1:
            y_a = _dot(act_ref[:, :half * tn], wd_ref[:half * tn, :])
    y = h_ref[0] + y_a + _dot(act_ref[:, half * tn:], wd_ref[half * tn:, :])
    o_ref[0] = _rms(y, fnw_ref[...])


def _ffn_call(hmid3, nw, w_up, cw, cb, w_down, fnw, nb, s_len):
    lm = hmid3.shape[1]
    tm = 512 if s_len % 512 == 0 else CHUNK
    tn = 256
    nh = lm // HALO
    hpt = tm // HALO

    def prev_blk(t):
        return jnp.where(t == 0, nh - 1, t * hpt - 1)

    def resident(shape):
        return pl.BlockSpec(shape, lambda b, t: (0,) * len(shape), pipeline_mode=pl.Buffered(1))

    return pl.pallas_call(
        functools.partial(_ffn_kernel, tm=tm, tn=tn),
        grid=(nb, s_len // tm),
        in_specs=[
            pl.BlockSpec((1, tm, D_MODEL), lambda b, t: (b, t, 0)),
            pl.BlockSpec((1, HALO, D_MODEL), lambda b, t: (b, prev_blk(t), 0)),
            pl.BlockSpec((1, HALO, D_MODEL), lambda b, t: (b, (t + 1) * hpt, 0)),
            resident((1, D_MODEL)),
            resident((D_MODEL, 2 * D_FF)),
            resident((3, 2 * D_FF)),
            resident((1, 2 * D_FF)),
            resident((D_FF, D_MODEL)),
            resident((1, D_MODEL)),
        ],
        out_specs=pl.BlockSpec((1, tm, D_MODEL), lambda b, t: (b, t, 0)),
        out_shape=jax.ShapeDtypeStruct((nb, s_len, D_MODEL), F32),
        scratch_shapes=[pltpu.VMEM((tm + 2 * HALO, D_MODEL), BF16), pltpu.VMEM((tm, D_FF), BF16)],
        compiler_params=pltpu.CompilerParams(
            dimension_semantics=("parallel", "parallel"), vmem_limit_bytes=VMEM_LIMIT),
        name="ffn",
    )(hmid3, hmid3, hmid3, nw, w_up, cw, cb, w_down, fnw)


def _position_tables(s_len):
    half = RET_QK_DIM // 2
    inv = ROPE_BASE ** (-jnp.arange(half, dtype=F32) / half)
    pos = jnp.concatenate([N_META + jnp.arange(s_len), jnp.zeros((PAD,), jnp.int32), jnp.arange(N_META)])
    ang = pos.astype(F32)[:, None] * inv[None, :]
    cos, sin = jnp.cos(ang), jnp.sin(ang)
    valid = jnp.concatenate([jnp.ones((s_len,), F32), jnp.zeros((PAD,), F32), jnp.ones((N_META,), F32)])
    return (jnp.concatenate([cos, cos], axis=1), jnp.concatenate([-sin, sin], axis=1),
            jnp.broadcast_to(valid[:, None], (s_len + CHUNK, LANES)))


def kernel(x, meta_tokens, norm_mix_w, w_in, ret_gn_w, w_ret_out, w_ssd_conv, b_ssd_conv, dt_bias_f, dt_bias_b, a_log_f, a_log_b, d_skip, ssd_norm_w, w_ssd_out, w_out, norm_ffn_w, w_ffn_up, w_ffn_conv, b_ffn_conv, w_ffn_down, final_norm_w):
    assert norm_mix_w.shape[0] == 1, "single-layer block"
    nb, s_len, _ = x.shape
    assert s_len % CHUNK == 0
    lm = s_len + CHUNK
    nc = lm // CHUNK

    tail = jnp.concatenate([jnp.zeros((PAD, D_MODEL), x.dtype), meta_tokens.astype(x.dtype)], axis=0)

    wi = w_in[0]
    dt0 = _W_MAIN_COLS
    w_a = wi[:, :dt0].astype(BF16)
    w_g = wi[:, dt0 + 2 * SSD_HEADS:].astype(BF16)
    w_dt = jnp.pad(wi[:, dt0:dt0 + 2 * SSD_HEADS], ((0, 0), (0, DT_W - 2 * SSD_HEADS))).astype(BF16)
    cos, sin, valid = _position_tables(s_len)

    proj3, dt3, hm = _proj_call(x, tail, norm_mix_w, w_a, w_g, w_dt, cos, sin, valid, w_ssd_conv[0], b_ssd_conv)

    lane_pad = jnp.zeros((1, DT_W - 2 * SSD_HEADS), F32)
    dtbias = jnp.concatenate([dt_bias_f, dt_bias_b, lane_pad], axis=1)
    alog = jnp.concatenate([a_log_f, a_log_b, lane_pad], axis=1)
    dskip = jnp.repeat(d_skip, SSD_HEAD_DIM, axis=1)
    y_ret, y_ssd = _mixers(proj3, dt3, ret_gn_w, dtbias, alog, dskip, ssd_norm_w, nb, nc)

    h_mid = _merge_call(hm, y_ret, y_ssd, proj3, w_ret_out[0].astype(BF16), w_ssd_out[0].astype(BF16),
                        w_out[0].astype(BF16))

    return _ffn_call(h_mid, norm_ffn_w, w_ffn_up[0].astype(BF16), w_ffn_conv[0],
                     b_ffn_conv, w_ffn_down[0].astype(BF16), final_norm_w.reshape(1, D_MODEL), nb, s_len)
```

```python
import functools
import math

import jax
import jax.numpy as jnp
import numpy as np
from jax import lax
from jax.experimental import pallas as pl
from jax.experimental.pallas import tpu as pltpu

F32 = jnp.float32
BF16 = jnp.bfloat16

D_MODEL = 1024
N_META = 16
CHUNK = 128
PAD = CHUNK - N_META
RET_HEADS = 4
RET_QK_DIM = 128
RET_V_DIM = 256
RET_QK = RET_HEADS * RET_QK_DIM
RET_V = RET_HEADS * RET_V_DIM
SSD_D_INNER = 2 * D_MODEL
SSD_HEAD_DIM = 64
SSD_HEADS = SSD_D_INNER // SSD_HEAD_DIM
SSD_GROUPS = 4
SSD_HPG = SSD_HEADS // SSD_GROUPS
SSD_STATE = 128
SSD_GW = SSD_HPG * SSD_HEAD_DIM
SSD_BC = SSD_GROUPS * SSD_STATE
SSD_XBC = SSD_D_INNER + 2 * SSD_BC
D_FF = 2816
EPS = 1e-6
ROPE_BASE = 10000.0
LOG_GAMMA = tuple(math.log(1.0 - 2.0 ** (-5.0 - h)) for h in range(RET_HEADS))
LOG2E = math.log2(math.e)

COL_Z = 0
COL_XS = 2048
COL_B = 4096
COL_C = 4608
COL_Q = 5120
COL_K = 5632
COL_V = 6144
COL_G = 7168
COL_GATES = 8192
PROJ_W = 10240
PROJ_TN = 1024
DT_W = 128

HALO = 16
LANES = 128
VMEM_LIMIT = 56 * 1024 * 1024


def _sigmoid(x):
    return 0.5 + 0.5 * jnp.tanh(0.5 * x)


def _silu(x):
    hx = 0.5 * x
    return hx + hx * jnp.tanh(hx)


def _softplus(x):
    return jnp.maximum(x, 0.0) + jnp.log1p(jnp.exp(-jnp.abs(x)))


def _split3(x):
    hi = x.astype(BF16)
    r1 = x - hi.astype(F32)
    mid = r1.astype(BF16)
    lo = (r1 - mid.astype(F32)).astype(BF16)
    return hi, mid, lo


def _dot(a, b):
    return jnp.dot(a, b, preferred_element_type=F32)


def _dot_tn(a, b):
    return lax.dot_general(a, b, (((0,), (0,)), ((), ())), preferred_element_type=F32)


def _dot_nt(a, b):
    return lax.dot_general(a, b, (((1,), (1,)), ((), ())), preferred_element_type=F32)


def _row_tile(rows, cap):
    best = CHUNK
    t = CHUNK
    while t <= min(rows, cap):
        if rows % t == 0:
            best = t
        t += CHUNK
    return best


def _rms(x, w):
    ms = jnp.mean(x * x, axis=-1, keepdims=True)
    return x * lax.rsqrt(ms + EPS) * w


def _conv_rows(f, w_ref, b_ref, cols, tm):
    prev = pltpu.roll(f, 1, 0)[HALO:HALO + tm]
    nxt = pltpu.roll(f, f.shape[0] - 1, 0)[HALO:HALO + tm]
    return (w_ref[0:1, cols] * prev + w_ref[1:2, cols] * f[HALO:HALO + tm] + w_ref[2:3, cols] * nxt
            + b_ref[:, cols])


_W_MAIN_COLS = 2 * RET_QK + 2 * RET_V + SSD_D_INNER + SSD_XBC


def _weight_source(c0):
    if c0 < COL_Q:
        return 0, c0 + 2 * RET_QK + 2 * RET_V
    if c0 < COL_GATES:
        return 0, c0 - COL_Q
    return 1, c0 - COL_GATES


def _proj_kernel(*refs, tm, n_tail_x, s_len):
    x_ref = refs[0]
    xc_refs = refs[1:1 + n_tail_x]
    (tail_ref, xp_ref, xn_ref, nw_ref, wa_ref, wg_ref, wdt_ref, cos_ref, sin_ref, valid_ref, cw_ref, cb_ref,
     o_ref, odt_ref, hm_ref, u_ref) = refs[1 + n_tail_x:]
    t = pl.program_id(1)
    is_last = t == pl.num_programs(1) - 1
    nw = nw_ref[...]

    def put(rows, x):
        hm_ref[0, rows, :] = x
        u_ref[HALO + rows.start:HALO + rows.stop, :] = _rms(x, nw).astype(BF16)

    @pl.when(jnp.logical_not(is_last))
    def _():
        put(slice(0, tm), x_ref[0])

    @pl.when(is_last)
    def _():
        for i, xc_ref in enumerate(xc_refs):
            put(slice(i * CHUNK, (i + 1) * CHUNK), xc_ref[0])
        put(slice(tm - CHUNK, tm), tail_ref[...])

    @pl.when(t == 0)
    def _():
        u_ref[0:HALO, :] = _rms(tail_ref[CHUNK - HALO:CHUNK, :], nw).astype(BF16)

    @pl.when(t != 0)
    def _():
        u_ref[0:HALO, :] = _rms(xp_ref[0], nw).astype(BF16)

    next_is_tail = (t + 1) * tm == s_len

    @pl.when(next_is_tail)
    def _():
        u_ref[HALO + tm:2 * HALO + tm, :] = _rms(tail_ref[0:HALO, :], nw).astype(BF16)

    @pl.when(jnp.logical_not(next_is_tail))
    def _():
        u_ref[HALO + tm:2 * HALO + tm, :] = _rms(xn_ref[0], nw).astype(BF16)

    u = u_ref[HALO:HALO + tm, :]
    odt_ref[0] = _dot(u, wdt_ref[...])
    cos = cos_ref[...]
    sin = sin_ref[...]
    valid = valid_ref[...]
    w_refs = (wa_ref, wg_ref)
    for c0 in range(0, PROJ_W, PROJ_TN):
        src, s0 = _weight_source(c0)
        w = w_refs[src][:, s0:s0 + PROJ_TN]
        if COL_XS <= c0 < COL_Q:
            f = _dot(u_ref[...], w)
            x0 = c0 - COL_XS
            act = _silu(_conv_rows(f, cw_ref, cb_ref, slice(x0, x0 + PROJ_TN), tm))
            for k in range(PROJ_TN // LANES):
                o_ref[0, :, c0 + k * LANES:c0 + (k + 1) * LANES] = (
                    act[:, k * LANES:(k + 1) * LANES] * valid).astype(BF16)
        elif COL_Q <= c0 < COL_V:
            acc = _dot(u, w)
            for hb in range(PROJ_TN // LANES):
                a = acc[:, hb * LANES:(hb + 1) * LANES]
                r = a * cos + pltpu.roll(a, RET_QK_DIM // 2, 1) * sin
                if c0 + hb * LANES >= COL_K:
                    r = r * (RET_QK_DIM ** -0.5)
                o_ref[0, :, c0 + hb * LANES:c0 + (hb + 1) * LANES] = r.astype(BF16)
        else:
            o_ref[0, :, c0:c0 + PROJ_TN] = _dot(u, w).astype(BF16)


def _proj_call(x, tail, nw, wa, wg, wdt, cos, sin, valid, cw, cb):
    nb, s_len, _ = x.shape
    lm = s_len + CHUNK
    tm = _row_tile(lm, 384)
    nt = lm // tm
    hpt = tm // HALO
    n_tail_x = tm // CHUNK - 1
    n_xh = s_len // HALO

    def resident(shape):
        return pl.BlockSpec(shape, lambda b, t: (0,) * len(shape), pipeline_mode=pl.Buffered(1))

    tail_x_specs = [
        pl.BlockSpec((1, CHUNK, D_MODEL), lambda b, t, i=i: (b, s_len // CHUNK - n_tail_x + i, 0))
        for i in range(n_tail_x)]

    return pl.pallas_call(
        functools.partial(_proj_kernel, tm=tm, n_tail_x=n_tail_x, s_len=s_len),
        grid=(nb, nt),
        in_specs=[pl.BlockSpec((1, tm, D_MODEL), lambda b, t: (b, jnp.minimum(t, nt - 2), 0))] + tail_x_specs + [
            resident((CHUNK, D_MODEL)),
            pl.BlockSpec((1, HALO, D_MODEL), lambda b, t: (b, jnp.maximum(t * hpt - 1, 0), 0)),
            pl.BlockSpec((1, HALO, D_MODEL),
                         lambda b, t: (b, jnp.where(t == nt - 1, 0, jnp.minimum((t + 1) * hpt, n_xh - 1)), 0)),
            resident((1, D_MODEL)),
            resident((D_MODEL, _W_MAIN_COLS)),
            resident((D_MODEL, 2 * D_MODEL)),
            resident((D_MODEL, DT_W)),
            pl.BlockSpec((tm, LANES), lambda b, t: (t, 0)),
            pl.BlockSpec((tm, LANES), lambda b, t: (t, 0)),
            pl.BlockSpec((tm, LANES), lambda b, t: (t, 0)),
            resident((3, SSD_XBC)),
            resident((1, SSD_XBC)),
        ],
        out_specs=[
            pl.BlockSpec((1, tm, PROJ_W), lambda b, t: (b, t, 0)),
            pl.BlockSpec((1, tm, DT_W), lambda b, t: (b, t, 0)),
            pl.BlockSpec((1, tm, D_MODEL), lambda b, t: (b, t, 0)),
        ],
        out_shape=[
            jax.ShapeDtypeStruct((nb, lm, PROJ_W), BF16),
            jax.ShapeDtypeStruct((nb, lm, DT_W), F32),
            jax.ShapeDtypeStruct((nb, lm, D_MODEL), F32),
        ],
        scratch_shapes=[pltpu.VMEM((tm + 2 * HALO, D_MODEL), BF16)],
        compiler_params=pltpu.CompilerParams(
            dimension_semantics=("parallel", "parallel"), vmem_limit_bytes=VMEM_LIMIT),
        name="in_proj",
    )(x, *([x] * n_tail_x), tail, x, x, nw, wa, wg, wdt, cos, sin, valid, cw, cb)


def _chunk_pos():
    return lax.broadcasted_iota(jnp.int32, (CHUNK, 1), 0).astype(F32)


def _ret_bstate_chunk(k_ref, v_ref, o_ref, r_ref):
    pos = _chunk_pos()
    for h in range(RET_HEADS):
        lg = LOG_GAMMA[h]
        r = r_ref[h]
        o_ref[h * RET_QK_DIM:(h + 1) * RET_QK_DIM, :] = r.astype(BF16)
        k = k_ref[:, h * RET_QK_DIM:(h + 1) * RET_QK_DIM].astype(F32)
        kd = (k * jnp.exp(lg * pos)).astype(BF16)
        v = v_ref[:, h * RET_V_DIM:(h + 1) * RET_V_DIM]
        r_ref[h] = math.exp(CHUNK * lg) * r + _dot_tn(kd, v)


def _ret_main_chunk(q_ref, k_ref, v_ref, g_ref, rb_ref, gnw_ref, o_ref, r_ref):
    pos = _chunk_pos()
    li = lax.broadcasted_iota(jnp.int32, (CHUNK, CHUNK), 0)
    si = lax.broadcasted_iota(jnp.int32, (CHUNK, CHUNK), 1)
    dist = jnp.abs(li - si).astype(F32)
    for h in range(RET_HEADS):
        lg = LOG_GAMMA[h]
        q = q_ref[:, h * RET_QK_DIM:(h + 1) * RET_QK_DIM]
        k = k_ref[:, h * RET_QK_DIM:(h + 1) * RET_QK_DIM]
        v = v_ref[:, h * RET_V_DIM:(h + 1) * RET_V_DIM]
        r = r_ref[h]
        s = _dot_nt(q, k) * jnp.exp(lg * dist)
        y = _dot(s.astype(BF16), v)
        y = y + jnp.exp(lg * (pos + 1.0)) * _dot(q, r.astype(BF16))
        rb = rb_ref[h * RET_QK_DIM:(h + 1) * RET_QK_DIM, :]
        y = y + jnp.exp(lg * (CHUNK - pos)) * _dot(q, rb)
        kd = (k.astype(F32) * jnp.exp(lg * (CHUNK - 1.0 - pos))).astype(BF16)
        r_ref[h] = math.exp(CHUNK * lg) * r + _dot_tn(kd, v)
        mu = jnp.mean(y, axis=-1, keepdims=True)
        yc = y - mu
        var = jnp.mean(yc * yc, axis=-1, keepdims=True)
        yn = yc * lax.rsqrt(var + EPS)
        sl = slice(h * RET_V_DIM, (h + 1) * RET_V_DIM)
        g = g_ref[:, sl].astype(F32)
        o_ref[:, sl] = (_silu(g) * (yn * gnw_ref[:, sl])).astype(BF16)


def _mem_block(c, nr):
    return jnp.where(c == 0, nr, c - 1)


def _valid_rows(is_meta_chunk):
    row = lax.broadcasted_iota(jnp.int32, (CHUNK, 1), 0)
    return jnp.where(jnp.logical_and(is_meta_chunk, row < PAD), 0.0, 1.0).astype(F32)


def _ssd_decays(dt_ref, dtb_ref, alog_ref, valid):
    dt = _softplus(dt_ref[...] + dtb_ref[...]) * valid
    da = dt * (-LOG2E * jnp.exp(alog_ref[...]))
    li = lax.broadcasted_iota(jnp.int32, (CHUNK, CHUNK), 0)
    si = lax.broadcasted_iota(jnp.int32, (CHUNK, CHUNK), 1)
    tri = jnp.where(si <= li, 1.0, 0.0).astype(BF16)
    hi, mid, lo = _split3(da)
    acs = _dot(tri, hi) + _dot(tri, mid) + _dot(tri, lo)
    return dt, da, acs


def _pack_hi_lo(x):
    lane = lax.broadcasted_iota(jnp.int32, x.shape, 1)
    hi = x.astype(BF16).astype(F32)
    return jnp.where(lane < LANES // 2, hi, pltpu.roll(x - hi, LANES // 2, 1)).astype(BF16)


def _ssd_bstate_chunk(xs_ref, b_ref, dt_ref, dtb_ref, alog_ref, eb_ref, o_ref, s_ref, valid):
    dt, da, acs = _ssd_decays(dt_ref, dtb_ref, alog_ref, valid)
    stack = jnp.concatenate([dt * jnp.exp2(acs - da), jnp.exp2(acs[CHUNK - HALO:CHUNK, :])], axis=0)
    ex = _dot(_pack_hi_lo(stack), eb_ref[...])
    xw = (xs_ref[...].astype(F32) * ex[0:CHUNK, :]).astype(BF16)
    cdec = ex[CHUNK + HALO - 1:CHUNK + HALO, :]
    bm = b_ref[...]
    for g in range(SSD_GROUPS):
        sl = slice(g * SSD_GW, (g + 1) * SSD_GW)
        s = s_ref[g]
        o_ref[g * SSD_STATE:(g + 1) * SSD_STATE, :] = s.astype(BF16)
        s_ref[g] = cdec[:, sl] * s + _dot_tn(bm[:, g * SSD_STATE:(g + 1) * SSD_STATE], xw[:, sl])


def _pack3(x, src):
    lane = lax.broadcasted_iota(jnp.int32, x.shape, 1)
    hi = x.astype(BF16).astype(F32)
    r1 = x - hi
    mid = r1.astype(BF16).astype(F32)
    lo = r1 - mid
    packed = jnp.where(lane < src + SSD_HEADS, hi,
                       jnp.where(lane < src + 2 * SSD_HEADS, pltpu.roll(mid, SSD_HEADS, 1),
                                 pltpu.roll(lo, 2 * SSD_HEADS, 1)))
    return packed.astype(BF16)


def _ssd_main_chunk(z_ref, xs_ref, bc_ref, dt_ref, sb_ref, dtb_ref, alog_ref, ef_ref, cf_ref,
                    dskip_ref, nw_ref, o_ref, s_ref, y_ref, valid):
    dt, da, acs = _ssd_decays(dt_ref, dtb_ref, alog_ref, valid)
    last = acs[CHUNK - 1:CHUNK, :]
    to_right = last - (acs - da)
    ldt_t = jnp.log2(dt).T
    row_f = acs.T - ldt_t
    row_b = ldt_t - to_right.T
    diag_t = jnp.log2(dt + pltpu.roll(dt, LANES - SSD_HEADS, 1)).T
    xs_b = xs_ref[...]
    bc = bc_ref[...]

    li = lax.broadcasted_iota(jnp.int32, (CHUNK, CHUNK), 0)
    si = lax.broadcasted_iota(jnp.int32, (CHUNK, CHUNK), 1)
    below = si < li
    above = si > li
    lane = lax.broadcasted_iota(jnp.int32, (CHUNK, LANES), 1)
    first_half = lane < SSD_HEAD_DIM
    zero = jnp.zeros((CHUNK, LANES), BF16)

    cb = []
    for g in range(SSD_GROUPS):
        bg = bc[:, g * SSD_STATE:(g + 1) * SSD_STATE]
        cg = bc[:, SSD_BC + g * SSD_STATE:SSD_BC + (g + 1) * SSD_STATE]
        cb.append(_dot_nt(cg, bg))

    col_f = _dot(_pack3(acs, 0), cf_ref[...])

    def mix(h, causal_col, anti_col):
        lb = SSD_HEADS + h
        causal = causal_col - row_f[h:h + 1, :]
        anti = row_b[lb:lb + 1, :] + anti_col
        e = jnp.where(below, causal, jnp.where(above, anti, diag_t[h:h + 1, :]))
        return (cb[h // SSD_HPG] * jnp.exp2(e)).astype(BF16)

    ea_f, ea_b = [], []
    for p in range(SSD_HEADS // 2):
        ha, hb = 2 * p, 2 * p + 1
        fa = col_f[:, ha * LANES:(ha + 1) * LANES]
        fb = col_f[:, hb * LANES:(hb + 1) * LANES]
        ra = jnp.broadcast_to(to_right[:, SSD_HEADS + ha:SSD_HEADS + ha + 1], (CHUNK, LANES))
        rb = jnp.broadcast_to(to_right[:, SSD_HEADS + hb:SSD_HEADS + hb + 1], (CHUNK, LANES))
        ea_f.append(jnp.exp2(jnp.where(first_half, fa, fb)))
        ea_b.append(jnp.exp2(jnp.where(first_half, ra, rb)))
        xp = xs_b[:, p * LANES:(p + 1) * LANES]
        rhs = jnp.concatenate([jnp.where(first_half, xp, zero), jnp.where(first_half, zero, xp)], axis=0)
        y_ref[:, p * LANES:(p + 1) * LANES] = _dot(jnp.concatenate([mix(ha, fa, ra), mix(hb, fb, rb)], axis=1), rhs)

    w_f = _dot(_pack_hi_lo(dt * jnp.exp2(last - acs)), ef_ref[...])
    xs = xs_b.astype(F32)
    xw = (xs * w_f).astype(BF16)
    tiles = SSD_GW // LANES
    for g in range(SSD_GROUPS):
        sl = slice(g * SSD_GW, (g + 1) * SSD_GW)
        bg = bc[:, g * SSD_STATE:(g + 1) * SSD_STATE]
        cg = bc[:, SSD_BC + g * SSD_STATE:SSD_BC + (g + 1) * SSD_STATE]
        ea_fg = jnp.concatenate(ea_f[g * tiles:(g + 1) * tiles], axis=1)
        ea_bg = jnp.concatenate(ea_b[g * tiles:(g + 1) * tiles], axis=1)
        s = s_ref[g]
        y_off = ea_fg * _dot(cg, s.astype(BF16))
        y_off = y_off + ea_bg * _dot(cg, sb_ref[g * SSD_STATE:(g + 1) * SSD_STATE, :])
        s_ref[g] = ea_fg[CHUNK - 1:CHUNK, :] * s + _dot_tn(bg, xw[:, sl])
        y = y_ref[:, sl] + y_off + xs[:, sl] * dskip_ref[:, sl]
        y = y * _silu(z_ref[:, sl].astype(F32))
        o_ref[:, sl] = _rms(y, nw_ref[:, sl]).astype(BF16)


def _head_expander(offset):
    e = np.zeros((LANES, SSD_D_INNER), np.float32)
    for h in range(SSD_HEADS):
        e[offset + h, h * SSD_HEAD_DIM:(h + 1) * SSD_HEAD_DIM] = 1.0
        e[LANES // 2 + offset + h, h * SSD_HEAD_DIM:(h + 1) * SSD_HEAD_DIM] = 1.0
    return jnp.asarray(e, BF16)


def _lane_broadcaster(src):
    e = np.zeros((LANES, SSD_HEADS * LANES), np.float32)
    for h in range(SSD_HEADS):
        for part in range(3):
            e[src + part * SSD_HEADS + h, h * LANES:(h + 1) * LANES] = 1.0
    return jnp.asarray(e, BF16)


def _bstate_kernel(*refs, g_chunks):
    n_in = 5 * g_chunks
    dtb_ref, alog_ref, eb_ref, rb_ref, sb_ref, r_ref, s_ref = refs[n_in:]
    step = pl.program_id(1)

    @pl.when(step == 0)
    def _():
        r_ref[...] = jnp.zeros_like(r_ref)
        s_ref[...] = jnp.zeros_like(s_ref)

    is_last = step == pl.num_programs(1) - 1
    for p in range(g_chunks):
        k_ref, v_ref, xs_ref, b_ref, dt_ref = refs[5 * p:5 * p + 5]
        slot = g_chunks - 1 - p
        _ret_bstate_chunk(k_ref.at[0], v_ref.at[0], rb_ref.at[0, slot], r_ref)
        valid = _valid_rows(is_last if slot == 0 else False)
        _ssd_bstate_chunk(xs_ref.at[0], b_ref.at[0], dt_ref.at[0], dtb_ref, alog_ref, eb_ref,
                          sb_ref.at[0, slot], s_ref, valid)


def _main_kernel(*refs, g_chunks):
    n_in = 8 * g_chunks
    rb_ref, sb_ref, gnw_ref, dtb_ref, alog_ref, ef_ref, cf_ref, dskip_ref, nw_ref = refs[n_in:n_in + 9]
    yr_refs = refs[n_in + 9:n_in + 9 + g_chunks]
    ys_refs = refs[n_in + 9 + g_chunks:n_in + 9 + 2 * g_chunks]
    r_ref, s_ref, y_ref = refs[n_in + 9 + 2 * g_chunks:]
    step = pl.program_id(1)

    @pl.when(step == 0)
    def _():
        r_ref[...] = jnp.zeros_like(r_ref)
        s_ref[...] = jnp.zeros_like(s_ref)

    for p in range(g_chunks):
        q_ref, k_ref, v_ref, g_ref, z_ref, xs_ref, bc_ref, dt_ref = refs[8 * p:8 * p + 8]
        _ret_main_chunk(q_ref.at[0], k_ref.at[0], v_ref.at[0], g_ref.at[0], rb_ref.at[0, p], gnw_ref,
                        yr_refs[p].at[0, 0], r_ref)
        valid = _valid_rows(step == 0 if p == 0 else False)
        _ssd_main_chunk(z_ref.at[0], xs_ref.at[0], bc_ref.at[0], dt_ref.at[0], sb_ref.at[0, p], dtb_ref, alog_ref,
                        ef_ref, cf_ref, dskip_ref, nw_ref, ys_refs[p].at[0, 0], s_ref,
                        y_ref.at[p], valid)


def _chunks_per_step(nc):
    return 3 if nc % 3 == 0 else 1


def _mixers(proj3, dt3, gnw, dtbias, alog, dskip, nw, nb, nc):
    nr = nc - 1
    gc = _chunks_per_step(nc)
    ng = nc // gc
    qb, vb = RET_QK, RET_V
    xw_, bw_, bcw_ = SSD_D_INNER, SSD_BC, 2 * SSD_BC
    e_f = _head_expander(0)
    e_b = _head_expander(SSD_HEADS)
    c_f = _lane_broadcaster(0)
    params = pltpu.CompilerParams(dimension_semantics=("parallel", "arbitrary"), vmem_limit_bytes=VMEM_LIMIT)

    def const(shape):
        return pl.BlockSpec(shape, lambda b, s: (0,) * len(shape))

    def chunk_spec(width, col, chunk_of):
        return pl.BlockSpec((1, CHUNK, width), lambda b, s: (b, _mem_block(chunk_of(s), nr), col // width))

    bwd_specs, bwd_args = [], []
    for p in range(gc):
        def chunk_of(s, p=p):
            return nc - 1 - (gc * s + p)
        bwd_specs += [chunk_spec(qb, COL_K, chunk_of), chunk_spec(vb, COL_V, chunk_of),
                      chunk_spec(xw_, COL_XS, chunk_of), chunk_spec(bw_, COL_B, chunk_of),
                      pl.BlockSpec((1, CHUNK, DT_W), lambda b, s, f=chunk_of: (b, _mem_block(f(s), nr), 0))]
        bwd_args += [proj3, proj3, proj3, proj3, dt3]

    rb, sb = pl.pallas_call(
        functools.partial(_bstate_kernel, g_chunks=gc),
        grid=(nb, ng),
        in_specs=bwd_specs + [const((1, DT_W)), const((1, DT_W)), const((LANES, SSD_D_INNER))],
        out_specs=[
            pl.BlockSpec((1, gc, RET_QK, RET_V_DIM), lambda b, s: (b, ng - 1 - s, 0, 0)),
            pl.BlockSpec((1, gc, SSD_BC, SSD_GW), lambda b, s: (b, ng - 1 - s, 0, 0)),
        ],
        out_shape=[
            jax.ShapeDtypeStruct((nb, nc, RET_QK, RET_V_DIM), BF16),
            jax.ShapeDtypeStruct((nb, nc, SSD_BC, SSD_GW), BF16),
        ],
        scratch_shapes=[
            pltpu.VMEM((RET_HEADS, RET_QK_DIM, RET_V_DIM), F32),
            pltpu.VMEM((SSD_GROUPS, SSD_STATE, SSD_GW), F32),
        ],
        compiler_params=params,
        name="mix_bwd_state",
    )(*bwd_args, dtbias, alog, e_b)

    fwd_specs, fwd_args = [], []
    for p in range(gc):
        def chunk_of(s, p=p):
            return gc * s + p
        fwd_specs += [chunk_spec(qb, COL_Q, chunk_of), chunk_spec(qb, COL_K, chunk_of),
                      chunk_spec(vb, COL_V, chunk_of), chunk_spec(vb, COL_G, chunk_of),
                      chunk_spec(xw_, COL_Z, chunk_of), chunk_spec(xw_, COL_XS, chunk_of),
                      chunk_spec(bcw_, COL_B, chunk_of),
                      pl.BlockSpec((1, CHUNK, DT_W), lambda b, s, f=chunk_of: (b, _mem_block(f(s), nr), 0))]
        fwd_args += [proj3] * 7 + [dt3]

    outs = pl.pallas_call(
        functools.partial(_main_kernel, g_chunks=gc),
        grid=(nb, ng),
        in_specs=fwd_specs + [
            pl.BlockSpec((1, gc, RET_QK, RET_V_DIM), lambda b, s: (b, s, 0, 0)),
            pl.BlockSpec((1, gc, SSD_BC, SSD_GW), lambda b, s: (b, s, 0, 0)),
            const((1, RET_V)), const((1, DT_W)), const((1, DT_W)),
            const((LANES, SSD_D_INNER)), const((LANES, SSD_HEADS * LANES)),
            const((1, SSD_D_INNER)), const((1, SSD_D_INNER)),
        ],
        out_specs=([pl.BlockSpec((1, 1, CHUNK, RET_V), lambda b, s: (b, s, 0, 0))] * gc
                   + [pl.BlockSpec((1, 1, CHUNK, SSD_D_INNER), lambda b, s: (b, s, 0, 0))] * gc),
        out_shape=([jax.ShapeDtypeStruct((nb, ng, CHUNK, RET_V), BF16)] * gc
                   + [jax.ShapeDtypeStruct((nb, ng, CHUNK, SSD_D_INNER), BF16)] * gc),
        scratch_shapes=[
            pltpu.VMEM((RET_HEADS, RET_QK_DIM, RET_V_DIM), F32),
            pltpu.VMEM((SSD_GROUPS, SSD_STATE, SSD_GW), F32),
            pltpu.VMEM((gc, CHUNK, SSD_D_INNER), F32),
        ],
        compiler_params=params,
        name="mix_main",
    )(*fwd_args, rb, sb, gnw, dtbias, alog, e_f, c_f, dskip, nw)
    return list(outs[:gc]), list(outs[gc:])


def _merge_kernel(*refs, g_chunks):
    h_ref = refs[0]
    yr_refs = refs[1:1 + g_chunks]
    ys_refs = refs[1 + g_chunks:1 + 2 * g_chunks]
    gates_ref, wr_ref, ws_ref, wo_ref, o_ref = refs[1 + 2 * g_chunks:]
    yr_in = jnp.concatenate([r[0, 0] for r in yr_refs], axis=0)
    ys_in = jnp.concatenate([r[0, 0] for r in ys_refs], axis=0)
    yr = _dot(yr_in, wr_ref[...])
    ys = _dot(ys_in, ws_ref[...])
    gr = _sigmoid(gates_ref[0, :, :D_MODEL].astype(F32))
    gs = _sigmoid(gates_ref[0, :, D_MODEL:].astype(F32))
    merged = (gr * yr + gs * ys).astype(BF16)
    o_ref[0] = h_ref[0] + _dot(merged, wo_ref[...])


def _merge_call(hm3, yr_list, ys_list, proj3, wr, ws, wo):
    nb, lm, _ = hm3.shape
    gc = len(yr_list)
    ng = yr_list[0].shape[1]
    tm = gc * CHUNK
    gw = 2 * D_MODEL

    def mixer_specs(arrays, width):
        specs, args = [], []
        for q in range(gc):
            if q < gc - 1:
                specs.append(pl.BlockSpec((1, 1, CHUNK, width), lambda b, t: (b, t, 0, 0)))
            else:
                specs.append(pl.BlockSpec((1, 1, CHUNK, width), lambda b, t: (b, lax.rem(t + 1, ng), 0, 0)))
            args.append(arrays[(q + 1) % gc])
        return specs, args

    yr_specs, yr_args = mixer_specs(yr_list, RET_V)
    ys_specs, ys_args = mixer_specs(ys_list, SSD_D_INNER)

    def resident(shape):
        return pl.BlockSpec(shape, lambda b, t: (0,) * len(shape))

    return pl.pallas_call(
        functools.partial(_merge_kernel, g_chunks=gc),
        grid=(nb, lm // tm),
        in_specs=[pl.BlockSpec((1, tm, D_MODEL), lambda b, t: (b, t, 0))] + yr_specs + ys_specs + [
            pl.BlockSpec((1, tm, gw), lambda b, t: (b, t, COL_GATES // gw)),
            resident((RET_V, D_MODEL)), resident((SSD_D_INNER, D_MODEL)), resident((D_MODEL, D_MODEL)),
        ],
        out_specs=pl.BlockSpec((1, tm, D_MODEL), lambda b, t: (b, t, 0)),
        out_shape=jax.ShapeDtypeStruct((nb, lm, D_MODEL), F32),
        compiler_params=pltpu.CompilerParams(
            dimension_semantics=("parallel", "parallel"), vmem_limit_bytes=VMEM_LIMIT),
        name="merge_out",
    )(hm3, *yr_args, *ys_args, proj3, wr, ws, wo)


def _ffn_kernel(h_ref, hp_ref, hn_ref, nw_ref, wup_ref, cw_ref, cb_ref, wd_ref, fnw_ref, o_ref, u_ref, act_ref,
                *, tm, tn):
    nw = nw_ref[...]
    u_ref[0:HALO, :] = _rms(hp_ref[0], nw).astype(BF16)
    u_ref[HALO:HALO + tm, :] = _rms(h_ref[0], nw).astype(BF16)
    u_ref[HALO + tm:2 * HALO + tm, :] = _rms(hn_ref[0], nw).astype(BF16)
    u = u_ref[...]
    nblk = D_FF // tn
    half = (nblk + 1) // 2

    def up(j):
        return (_dot(u, wup_ref[:, j * tn:(j + 1) * tn]),
                _dot(u, wup_ref[:, D_FF + j * tn:D_FF + (j + 1) * tn]))

    nxt = up(0)
    for j in range(nblk):
        fg, fu = nxt
        if j + 1 < nblk:
            nxt = up(j + 1)
        gc = slice(j * tn, (j + 1) * tn)
        uc = slice(D_FF + j * tn, D_FF + (j + 1) * tn)
        act = _silu(_conv_rows(fg, cw_ref, cb_ref, gc, tm)) * _conv_rows(fu, cw_ref, cb_ref, uc, tm)
        act_ref[:, gc] = act.astype(BF16)
        if j == half - 1:
            y_a = _dot(act_ref[:, :half * tn], wd_ref[:half * tn, :])
    y = h_ref[0] + y_a + _dot(act_ref[:, half * tn:], wd_ref[half * tn:, :])
    o_ref[0] = _rms(y, fnw_ref[...])


def _ffn_call(hmid3, nw, w_up, cw, cb, w_down, fnw, nb, s_len):
    lm = hmid3.shape[1]
    tm = 1024 if s_len % 1024 == 0 else CHUNK
    tn = 256
    nh = lm // HALO
    hpt = tm // HALO

    def prev_blk(t):
        return jnp.where(t == 0, nh - 1, t * hpt - 1)

    def resident(shape):
        return pl.BlockSpec(shape, lambda b, t: (0,) * len(shape), pipeline_mode=pl.Buffered(1))

    return pl.pallas_call(
        functools.partial(_ffn_kernel, tm=tm, tn=tn),
        grid=(nb, s_len // tm),
        in_specs=[
            pl.BlockSpec((1, tm, D_MODEL), lambda b, t: (b, t, 0)),
            pl.BlockSpec((1, HALO, D_MODEL), lambda b, t: (b, prev_blk(t), 0)),
            pl.BlockSpec((1, HALO, D_MODEL), lambda b, t: (b, (t + 1) * hpt, 0)),
            resident((1, D_MODEL)),
            resident((D_MODEL, 2 * D_FF)),
            resident((3, 2 * D_FF)),
            resident((1, 2 * D_FF)),
            resident((D_FF, D_MODEL)),
            resident((1, D_MODEL)),
        ],
        out_specs=pl.BlockSpec((1, tm, D_MODEL), lambda b, t: (b, t, 0)),
        out_shape=jax.ShapeDtypeStruct((nb, s_len, D_MODEL), F32),
        scratch_shapes=[pltpu.VMEM((tm + 2 * HALO, D_MODEL), BF16), pltpu.VMEM((tm, D_FF), BF16)],
        compiler_params=pltpu.CompilerParams(
            dimension_semantics=("parallel", "parallel"), vmem_limit_bytes=VMEM_LIMIT),
        name="ffn",
    )(hmid3, hmid3, hmid3, nw, w_up, cw, cb, w_down, fnw)


def _position_tables(s_len):
    half = RET_QK_DIM // 2
    inv = ROPE_BASE ** (-jnp.arange(half, dtype=F32) / half)
    pos = jnp.concatenate([N_META + jnp.arange(s_len), jnp.zeros((PAD,), jnp.int32), jnp.arange(N_META)])
    ang = pos.astype(F32)[:, None] * inv[None, :]
    cos, sin = jnp.cos(ang), jnp.sin(ang)
    valid = jnp.concatenate([jnp.ones((s_len,), F32), jnp.zeros((PAD,), F32), jnp.ones((N_META,), F32)])
    return (jnp.concatenate([cos, cos], axis=1), jnp.concatenate([-sin, sin], axis=1),
            jnp.broadcast_to(valid[:, None], (s_len + CHUNK, LANES)))


def kernel(x, meta_tokens, norm_mix_w, w_in, ret_gn_w, w_ret_out, w_ssd_conv, b_ssd_conv, dt_bias_f, dt_bias_b, a_log_f, a_log_b, d_skip, ssd_norm_w, w_ssd_out, w_out, norm_ffn_w, w_ffn_up, w_ffn_conv, b_ffn_conv, w_ffn_down, final_norm_w):
    assert norm_mix_w.shape[0] == 1, "single-layer block"
    nb, s_len, _ = x.shape
    assert s_len % CHUNK == 0
    lm = s_len + CHUNK
    nc = lm // CHUNK

    tail = jnp.concatenate([jnp.zeros((PAD, D_MODEL), x.dtype), meta_tokens.astype(x.dtype)], axis=0)

    wi = w_in[0]
    dt0 = _W_MAIN_COLS
    w_a = wi[:, :dt0].astype(BF16)
    w_g = wi[:, dt0 + 2 * SSD_HEADS:].astype(BF16)
    w_dt = jnp.pad(wi[:, dt0:dt0 + 2 * SSD_HEADS], ((0, 0), (0, DT_W - 2 * SSD_HEADS))).astype(BF16)
    cos, sin, valid = _position_tables(s_len)

    proj3, dt3, hm = _proj_call(x, tail, norm_mix_w, w_a, w_g, w_dt, cos, sin, valid, w_ssd_conv[0], b_ssd_conv)

    lane_pad = jnp.zeros((1, DT_W - 2 * SSD_HEADS), F32)
    dtbias = jnp.concatenate([dt_bias_f, dt_bias_b, lane_pad], axis=1)
    alog = jnp.concatenate([a_log_f, a_log_b, lane_pad], axis=1)
    dskip = jnp.repeat(d_skip, SSD_HEAD_DIM, axis=1)
    y_ret, y_ssd = _mixers(proj3, dt3, ret_gn_w, dtbias, alog, dskip, ssd_norm_w, nb, nc)

    h_mid = _merge_call(hm, y_ret, y_ssd, proj3, w_ret_out[0].astype(BF16), w_ssd_out[0].astype(BF16),
                        w_out[0].astype(BF16))

    return _ffn_call(h_mid, norm_ffn_w, w_ffn_up[0].astype(BF16), w_ffn_conv[0],
                     b_ffn_conv, w_ffn_down[0].astype(BF16), final_norm_w.reshape(1, D_MODEL), nb, s_len)
```

```python
import functools
import math

import jax
import jax.numpy as jnp
import numpy as np
from jax import lax
from jax.experimental import pallas as pl
from jax.experimental.pallas import tpu as pltpu

F32 = jnp.float32
BF16 = jnp.bfloat16

D_MODEL = 1024
N_META = 16
CHUNK = 128
PAD = CHUNK - N_META
RET_HEADS = 4
RET_QK_DIM = 128
RET_V_DIM = 256
RET_QK = RET_HEADS * RET_QK_DIM
RET_V = RET_HEADS * RET_V_DIM
SSD_D_INNER = 2 * D_MODEL
SSD_HEAD_DIM = 64
SSD_HEADS = SSD_D_INNER // SSD_HEAD_DIM
SSD_GROUPS = 4
SSD_HPG = SSD_HEADS // SSD_GROUPS
SSD_STATE = 128
SSD_GW = SSD_HPG * SSD_HEAD_DIM
SSD_BC = SSD_GROUPS * SSD_STATE
SSD_XBC = SSD_D_INNER + 2 * SSD_BC
D_FF = 2816
EPS = 1e-6
ROPE_BASE = 10000.0
LOG_GAMMA = tuple(math.log(1.0 - 2.0 ** (-5.0 - h)) for h in range(RET_HEADS))
LOG2E = math.log2(math.e)

COL_Z = 0
COL_XS = 2048
COL_B = 4096
COL_Q = 5120
COL_K = 5632
COL_V = 6144
COL_G = 7168
COL_GATES = 8192
PROJ_W = 10240
PROJ_TN = 1024
DT_W = 128

HALO = 16
LANES = 128
VMEM_LIMIT = 56 * 1024 * 1024


def _sigmoid(x):
    return 0.5 + 0.5 * jnp.tanh(0.5 * x)


def _silu(x):
    hx = 0.5 * x
    return hx + hx * jnp.tanh(hx)


def _softplus(x):
    return jnp.maximum(x, 0.0) + jnp.log1p(jnp.exp(-jnp.abs(x)))


def _split3(x):
    hi = x.astype(BF16)
    r1 = x - hi.astype(F32)
    mid = r1.astype(BF16)
    lo = (r1 - mid.astype(F32)).astype(BF16)
    return hi, mid, lo


def _dot(a, b):
    return jnp.dot(a, b, preferred_element_type=F32)


def _dot_tn(a, b):
    return lax.dot_general(a, b, (((0,), (0,)), ((), ())), preferred_element_type=F32)


def _dot_nt(a, b):
    return lax.dot_general(a, b, (((1,), (1,)), ((), ())), preferred_element_type=F32)


def _row_tile(rows, cap):
    best = CHUNK
    t = CHUNK
    while t <= min(rows, cap):
        if rows % t == 0:
            best = t
        t += CHUNK
    return best


def _rms(x, w):
    ms = jnp.mean(x * x, axis=-1, keepdims=True)
    return x * lax.rsqrt(ms + EPS) * w


def _conv_rows(f, w_ref, b_ref, cols, tm):
    prev = pltpu.roll(f, 1, 0)[HALO:HALO + tm]
    nxt = pltpu.roll(f, f.shape[0] - 1, 0)[HALO:HALO + tm]
    return (w_ref[0:1, cols] * prev + w_ref[1:2, cols] * f[HALO:HALO + tm] + w_ref[2:3, cols] * nxt
            + b_ref[:, cols])


_W_MAIN_COLS = 2 * RET_QK + 2 * RET_V + SSD_D_INNER + SSD_XBC


def _weight_source(c0):
    if c0 < COL_Q:
        return 0, c0 + 2 * RET_QK + 2 * RET_V
    if c0 < COL_GATES:
        return 0, c0 - COL_Q
    return 1, c0 - COL_GATES


def _proj_kernel(*refs, tm, n_tail_x, s_len):
    x_ref = refs[0]
    xc_refs = refs[1:1 + n_tail_x]
    (tail_ref, xp_ref, xn_ref, nw_ref, wa_ref, wg_ref, wdt_ref, cos_ref, sin_ref, valid_ref, cw_ref, cb_ref,
     o_ref, odt_ref, hm_ref, u_ref) = refs[1 + n_tail_x:]
    t = pl.program_id(1)
    is_last = t == pl.num_programs(1) - 1
    nw = nw_ref[...]

    tail = tail_ref[...]
    last_rows = jnp.concatenate([xc_ref[0] for xc_ref in xc_refs] + [tail], axis=0)
    rows = jnp.where(is_last, last_rows, x_ref[0])
    hm_ref[0] = rows
    prev = jnp.where(t == 0, tail[CHUNK - HALO:CHUNK, :], xp_ref[0])
    nxt = jnp.where((t + 1) * tm == s_len, tail[0:HALO, :], xn_ref[0])
    u_ref[0:HALO, :] = _rms(prev, nw).astype(BF16)
    u_ref[HALO:HALO + tm, :] = _rms(rows, nw).astype(BF16)
    u_ref[HALO + tm:2 * HALO + tm, :] = _rms(nxt, nw).astype(BF16)

    u = u_ref[HALO:HALO + tm, :]
    odt_ref[0] = _dot(u, wdt_ref[...])
    cos = cos_ref[...]
    sin = sin_ref[...]
    valid = valid_ref[...]
    w_refs = (wa_ref, wg_ref)
    for c0 in range(0, PROJ_W, PROJ_TN):
        src, s0 = _weight_source(c0)
        w = w_refs[src][:, s0:s0 + PROJ_TN]
        if COL_XS <= c0 < COL_Q:
            f = _dot(u_ref[...], w)
            x0 = c0 - COL_XS
            act = _silu(_conv_rows(f, cw_ref, cb_ref, slice(x0, x0 + PROJ_TN), tm))
            for k in range(PROJ_TN // LANES):
                o_ref[0, :, c0 + k * LANES:c0 + (k + 1) * LANES] = (
                    act[:, k * LANES:(k + 1) * LANES] * valid).astype(BF16)
        elif COL_Q <= c0 < COL_V:
            acc = _dot(u, w)
            for hb in range(PROJ_TN // LANES):
                a = acc[:, hb * LANES:(hb + 1) * LANES]
                r = a * cos + pltpu.roll(a, RET_QK_DIM // 2, 1) * sin
                if c0 + hb * LANES >= COL_K:
                    r = r * (RET_QK_DIM ** -0.5)
                o_ref[0, :, c0 + hb * LANES:c0 + (hb + 1) * LANES] = r.astype(BF16)
        else:
            o_ref[0, :, c0:c0 + PROJ_TN] = _dot(u, w).astype(BF16)


def _proj_call(x, tail, nw, wa, wg, wdt, cos, sin, valid, cw, cb):
    nb, s_len, _ = x.shape
    lm = s_len + CHUNK
    tm = _row_tile(lm, 384)
    nt = lm // tm
    hpt = tm // HALO
    n_tail_x = tm // CHUNK - 1
    n_xh = s_len // HALO

    def resident(shape):
        return pl.BlockSpec(shape, lambda b, t: (0,) * len(shape), pipeline_mode=pl.Buffered(1))

    tail_x_specs = [
        pl.BlockSpec((1, CHUNK, D_MODEL), lambda b, t, i=i: (b, s_len // CHUNK - n_tail_x + i, 0))
        for i in range(n_tail_x)]

    return pl.pallas_call(
        functools.partial(_proj_kernel, tm=tm, n_tail_x=n_tail_x, s_len=s_len),
        grid=(nb, nt),
        in_specs=[pl.BlockSpec((1, tm, D_MODEL), lambda b, t: (b, jnp.minimum(t, nt - 2), 0))] + tail_x_specs + [
            resident((CHUNK, D_MODEL)),
            pl.BlockSpec((1, HALO, D_MODEL), lambda b, t: (b, jnp.maximum(t * hpt - 1, 0), 0)),
            pl.BlockSpec((1, HALO, D_MODEL),
                         lambda b, t: (b, jnp.where(t == nt - 1, 0, jnp.minimum((t + 1) * hpt, n_xh - 1)), 0)),
            resident((1, D_MODEL)),
            resident((D_MODEL, _W_MAIN_COLS)),
            resident((D_MODEL, 2 * D_MODEL)),
            resident((D_MODEL, DT_W)),
            pl.BlockSpec((tm, LANES), lambda b, t: (t, 0)),
            pl.BlockSpec((tm, LANES), lambda b, t: (t, 0)),
            pl.BlockSpec((tm, LANES), lambda b, t: (t, 0)),
            resident((3, SSD_XBC)),
            resident((1, SSD_XBC)),
        ],
        out_specs=[
            pl.BlockSpec((1, tm, PROJ_W), lambda b, t: (b, t, 0)),
            pl.BlockSpec((1, tm, DT_W), lambda b, t: (b, t, 0)),
            pl.BlockSpec((1, tm, D_MODEL), lambda b, t: (b, t, 0)),
        ],
        out_shape=[
            jax.ShapeDtypeStruct((nb, lm, PROJ_W), BF16),
            jax.ShapeDtypeStruct((nb, lm, DT_W), F32),
            jax.ShapeDtypeStruct((nb, lm, D_MODEL), F32),
        ],
        scratch_shapes=[pltpu.VMEM((tm + 2 * HALO, D_MODEL), BF16)],
        compiler_params=pltpu.CompilerParams(
            dimension_semantics=("parallel", "parallel"), vmem_limit_bytes=VMEM_LIMIT),
        name="in_proj",
    )(x, *([x] * n_tail_x), tail, x, x, nw, wa, wg, wdt, cos, sin, valid, cw, cb)


def _chunk_pos():
    return lax.broadcasted_iota(jnp.int32, (CHUNK, 1), 0).astype(F32)


def _ret_bstate_chunk(k_ref, v_ref, o_ref, r_ref):
    pos = _chunk_pos()
    for h in range(RET_HEADS):
        lg = LOG_GAMMA[h]
        r = r_ref[h]
        o_ref[h * RET_QK_DIM:(h + 1) * RET_QK_DIM, :] = r.astype(BF16)
        k = k_ref[:, h * RET_QK_DIM:(h + 1) * RET_QK_DIM].astype(F32)
        kd = (k * jnp.exp(lg * pos)).astype(BF16)
        v = v_ref[:, h * RET_V_DIM:(h + 1) * RET_V_DIM]
        r_ref[h] = math.exp(CHUNK * lg) * r + _dot_tn(kd, v)


def _ret_main_chunk(q_ref, k_ref, v_ref, g_ref, rb_ref, gnw_ref, o_ref, r_ref):
    pos = _chunk_pos()
    li = lax.broadcasted_iota(jnp.int32, (CHUNK, CHUNK), 0)
    si = lax.broadcasted_iota(jnp.int32, (CHUNK, CHUNK), 1)
    dist = jnp.abs(li - si).astype(F32)
    for h in range(RET_HEADS):
        lg = LOG_GAMMA[h]
        q = q_ref[:, h * RET_QK_DIM:(h + 1) * RET_QK_DIM]
        k = k_ref[:, h * RET_QK_DIM:(h + 1) * RET_QK_DIM]
        v = v_ref[:, h * RET_V_DIM:(h + 1) * RET_V_DIM]
        r = r_ref[h]
        s = _dot_nt(q, k) * jnp.exp(lg * dist)
        y = _dot(s.astype(BF16), v)
        y = y + jnp.exp(lg * (pos + 1.0)) * _dot(q, r.astype(BF16))
        rb = rb_ref[h * RET_QK_DIM:(h + 1) * RET_QK_DIM, :]
        y = y + jnp.exp(lg * (CHUNK - pos)) * _dot(q, rb)
        kd = (k.astype(F32) * jnp.exp(lg * (CHUNK - 1.0 - pos))).astype(BF16)
        r_ref[h] = math.exp(CHUNK * lg) * r + _dot_tn(kd, v)
        mu = jnp.mean(y, axis=-1, keepdims=True)
        yc = y - mu
        var = jnp.mean(yc * yc, axis=-1, keepdims=True)
        yn = yc * lax.rsqrt(var + EPS)
        sl = slice(h * RET_V_DIM, (h + 1) * RET_V_DIM)
        g = g_ref[:, sl].astype(F32)
        o_ref[:, sl] = (_silu(g) * (yn * gnw_ref[:, sl])).astype(BF16)


def _mem_block(c, nr):
    return jnp.where(c == 0, nr, c - 1)


def _valid_rows(is_meta_chunk):
    row = lax.broadcasted_iota(jnp.int32, (CHUNK, 1), 0)
    return jnp.where(jnp.logical_and(is_meta_chunk, row < PAD), 0.0, 1.0).astype(F32)


def _ssd_decays(dt_ref, dtb_ref, alog_ref, valid):
    dt = _softplus(dt_ref[...] + dtb_ref[...]) * valid
    da = dt * (-LOG2E * jnp.exp(alog_ref[...]))
    li = lax.broadcasted_iota(jnp.int32, (CHUNK, CHUNK), 0)
    si = lax.broadcasted_iota(jnp.int32, (CHUNK, CHUNK), 1)
    tri = jnp.where(si <= li, 1.0, 0.0).astype(BF16)
    hi, mid, lo = _split3(da)
    acs = _dot(tri, hi) + _dot(tri, mid) + _dot(tri, lo)
    return dt, da, acs


def _pack_hi_lo(x):
    lane = lax.broadcasted_iota(jnp.int32, x.shape, 1)
    hi = x.astype(BF16).astype(F32)
    return jnp.where(lane < LANES // 2, hi, pltpu.roll(x - hi, LANES // 2, 1)).astype(BF16)


def _ssd_bstate_chunk(xs_ref, b_ref, dt_ref, dtb_ref, alog_ref, eb_ref, o_ref, s_ref, valid):
    dt, da, acs = _ssd_decays(dt_ref, dtb_ref, alog_ref, valid)
    stack = jnp.concatenate([dt * jnp.exp2(acs - da), jnp.exp2(acs[CHUNK - HALO:CHUNK, :])], axis=0)
    ex = _dot(_pack_hi_lo(stack), eb_ref[...])
    xw = (xs_ref[...].astype(F32) * ex[0:CHUNK, :]).astype(BF16)
    cdec = ex[CHUNK + HALO - 1:CHUNK + HALO, :]
    bm = b_ref[...]
    for g in range(SSD_GROUPS):
        sl = slice(g * SSD_GW, (g + 1) * SSD_GW)
        s = s_ref[g]
        o_ref[g * SSD_STATE:(g + 1) * SSD_STATE, :] = s.astype(BF16)
        s_ref[g] = cdec[:, sl] * s + _dot_tn(bm[:, g * SSD_STATE:(g + 1) * SSD_STATE], xw[:, sl])


def _pack3(x, src):
    lane = lax.broadcasted_iota(jnp.int32, x.shape, 1)
    hi = x.astype(BF16).astype(F32)
    r1 = x - hi
    mid = r1.astype(BF16).astype(F32)
    lo = r1 - mid
    packed = jnp.where(lane < src + SSD_HEADS, hi,
                       jnp.where(lane < src + 2 * SSD_HEADS, pltpu.roll(mid, SSD_HEADS, 1),
                                 pltpu.roll(lo, 2 * SSD_HEADS, 1)))
    return packed.astype(BF16)


def _ssd_main_chunk(z_ref, xs_ref, bc_ref, dt_ref, sb_ref, dtb_ref, alog_ref, ef_ref, cf_ref,
                    dskip_ref, nw_ref, o_ref, s_ref, y_ref, valid):
    dt, da, acs = _ssd_decays(dt_ref, dtb_ref, alog_ref, valid)
    last = acs[CHUNK - 1:CHUNK, :]
    to_right = last - (acs - da)
    ldt_t = jnp.log2(dt).T
    row_f = acs.T - ldt_t
    row_b = ldt_t - to_right.T
    diag_t = jnp.log2(dt + pltpu.roll(dt, LANES - SSD_HEADS, 1)).T
    xs_b = xs_ref[...]
    bc = bc_ref[...]

    li = lax.broadcasted_iota(jnp.int32, (CHUNK, CHUNK), 0)
    si = lax.broadcasted_iota(jnp.int32, (CHUNK, CHUNK), 1)
    below = si < li
    above = si > li
    lane = lax.broadcasted_iota(jnp.int32, (CHUNK, LANES), 1)
    first_half = lane < SSD_HEAD_DIM
    zero = jnp.zeros((CHUNK, LANES), BF16)

    cb = []
    for g in range(SSD_GROUPS):
        bg = bc[:, g * SSD_STATE:(g + 1) * SSD_STATE]
        cg = bc[:, SSD_BC + g * SSD_STATE:SSD_BC + (g + 1) * SSD_STATE]
        cb.append(_dot_nt(cg, bg))

    col_f = _dot(_pack3(acs, 0), cf_ref[...])

    def mix(h, causal_col, anti_col):
        lb = SSD_HEADS + h
        causal = causal_col - row_f[h:h + 1, :]
        anti = row_b[lb:lb + 1, :] + anti_col
        e = jnp.where(below, causal, jnp.where(above, anti, diag_t[h:h + 1, :]))
        return (cb[h // SSD_HPG] * jnp.exp2(e)).astype(BF16)

    ea_f, ea_b = [], []
    for p in range(SSD_HEADS // 2):
        ha, hb = 2 * p, 2 * p + 1
        fa = col_f[:, ha * LANES:(ha + 1) * LANES]
        fb = col_f[:, hb * LANES:(hb + 1) * LANES]
        ra = jnp.broadcast_to(to_right[:, SSD_HEADS + ha:SSD_HEADS + ha + 1], (CHUNK, LANES))
        rb = jnp.broadcast_to(to_right[:, SSD_HEADS + hb:SSD_HEADS + hb + 1], (CHUNK, LANES))
        ea_f.append(jnp.exp2(jnp.where(first_half, fa, fb)))
        ea_b.append(jnp.exp2(jnp.where(first_half, ra, rb)))
        xp = xs_b[:, p * LANES:(p + 1) * LANES]
        rhs = jnp.concatenate([jnp.where(first_half, xp, zero), jnp.where(first_half, zero, xp)], axis=0)
        y_ref[:, p * LANES:(p + 1) * LANES] = _dot(jnp.concatenate([mix(ha, fa, ra), mix(hb, fb, rb)], axis=1), rhs)

    w_f = _dot(_pack_hi_lo(dt * jnp.exp2(last - acs)), ef_ref[...])
    xs = xs_b.astype(F32)
    xw = (xs * w_f).astype(BF16)
    tiles = SSD_GW // LANES
    for g in range(SSD_GROUPS):
        sl = slice(g * SSD_GW, (g + 1) * SSD_GW)
        bg = bc[:, g * SSD_STATE:(g + 1) * SSD_STATE]
        cg = bc[:, SSD_BC + g * SSD_STATE:SSD_BC + (g + 1) * SSD_STATE]
        ea_fg = jnp.concatenate(ea_f[g * tiles:(g + 1) * tiles], axis=1)
        ea_bg = jnp.concatenate(ea_b[g * tiles:(g + 1) * tiles], axis=1)
        s = s_ref[g]
        y_off = ea_fg * _dot(cg, s.astype(BF16))
        y_off = y_off + ea_bg * _dot(cg, sb_ref[g * SSD_STATE:(g + 1) * SSD_STATE, :])
        s_ref[g] = ea_fg[CHUNK - 1:CHUNK, :] * s + _dot_tn(bg, xw[:, sl])
        y = y_ref[:, sl] + y_off + xs[:, sl] * dskip_ref[:, sl]
        y = y * _silu(z_ref[:, sl].astype(F32))
        o_ref[:, sl] = _rms(y, nw_ref[:, sl]).astype(BF16)


def _head_expander(offset):
    e = np.zeros((LANES, SSD_D_INNER), np.float32)
    for h in range(SSD_HEADS):
        e[offset + h, h * SSD_HEAD_DIM:(h + 1) * SSD_HEAD_DIM] = 1.0
        e[LANES // 2 + offset + h, h * SSD_HEAD_DIM:(h + 1) * SSD_HEAD_DIM] = 1.0
    return jnp.asarray(e, BF16)


def _lane_broadcaster(src):
    e = np.zeros((LANES, SSD_HEADS * LANES), np.float32)
    for h in range(SSD_HEADS):
        for part in range(3):
            e[src + part * SSD_HEADS + h, h * LANES:(h + 1) * LANES] = 1.0
    return jnp.asarray(e, BF16)


def _bstate_kernel(*refs, g_chunks):
    n_in = 5 * g_chunks
    dtb_ref, alog_ref, eb_ref, rb_ref, sb_ref, r_ref, s_ref = refs[n_in:]
    step = pl.program_id(1)

    @pl.when(step == 0)
    def _():
        r_ref[...] = jnp.zeros_like(r_ref)
        s_ref[...] = jnp.zeros_like(s_ref)

    is_last = step == pl.num_programs(1) - 1
    for p in range(g_chunks):
        k_ref, v_ref, xs_ref, b_ref, dt_ref = refs[5 * p:5 * p + 5]
        slot = g_chunks - 1 - p
        _ret_bstate_chunk(k_ref.at[0], v_ref.at[0], rb_ref.at[0, slot], r_ref)
        valid = _valid_rows(is_last if slot == 0 else False)
        _ssd_bstate_chunk(xs_ref.at[0], b_ref.at[0], dt_ref.at[0], dtb_ref, alog_ref, eb_ref,
                          sb_ref.at[0, slot], s_ref, valid)


def _main_kernel(*refs, g_chunks):
    n_in = 8 * g_chunks
    rb_ref, sb_ref, gnw_ref, dtb_ref, alog_ref, ef_ref, cf_ref, dskip_ref, nw_ref = refs[n_in:n_in + 9]
    yr_refs = refs[n_in + 9:n_in + 9 + g_chunks]
    ys_refs = refs[n_in + 9 + g_chunks:n_in + 9 + 2 * g_chunks]
    r_ref, s_ref, y_ref = refs[n_in + 9 + 2 * g_chunks:]
    step = pl.program_id(1)

    @pl.when(step == 0)
    def _():
        r_ref[...] = jnp.zeros_like(r_ref)
        s_ref[...] = jnp.zeros_like(s_ref)

    for p in range(g_chunks):
        q_ref, k_ref, v_ref, g_ref, z_ref, xs_ref, bc_ref, dt_ref = refs[8 * p:8 * p + 8]
        _ret_main_chunk(q_ref.at[0], k_ref.at[0], v_ref.at[0], g_ref.at[0], rb_ref.at[0, p], gnw_ref,
                        yr_refs[p].at[0, 0], r_ref)
        valid = _valid_rows(step == 0 if p == 0 else False)
        _ssd_main_chunk(z_ref.at[0], xs_ref.at[0], bc_ref.at[0], dt_ref.at[0], sb_ref.at[0, p], dtb_ref, alog_ref,
                        ef_ref, cf_ref, dskip_ref, nw_ref, ys_refs[p].at[0, 0], s_ref,
                        y_ref.at[p], valid)


def _chunks_per_step(nc):
    return 3 if nc % 3 == 0 else 1


def _mixers(proj3, dt3, gnw, dtbias, alog, dskip, nw, nb, nc):
    nr = nc - 1
    gc = _chunks_per_step(nc)
    ng = nc // gc
    qb, vb = RET_QK, RET_V
    xw_, bw_, bcw_ = SSD_D_INNER, SSD_BC, 2 * SSD_BC
    e_f = _head_expander(0)
    e_b = _head_expander(SSD_HEADS)
    c_f = _lane_broadcaster(0)
    params = pltpu.CompilerParams(dimension_semantics=("parallel", "arbitrary"), vmem_limit_bytes=VMEM_LIMIT)

    def const(shape):
        return pl.BlockSpec(shape, lambda b, s: (0,) * len(shape))

    def chunk_spec(width, col, chunk_of):
        return pl.BlockSpec((1, CHUNK, width), lambda b, s: (b, _mem_block(chunk_of(s), nr), col // width))

    bwd_specs, bwd_args = [], []
    for p in range(gc):
        def chunk_of(s, p=p):
            return nc - 1 - (gc * s + p)
        bwd_specs += [chunk_spec(qb, COL_K, chunk_of), chunk_spec(vb, COL_V, chunk_of),
                      chunk_spec(xw_, COL_XS, chunk_of), chunk_spec(bw_, COL_B, chunk_of),
                      pl.BlockSpec((1, CHUNK, DT_W), lambda b, s, f=chunk_of: (b, _mem_block(f(s), nr), 0))]
        bwd_args += [proj3, proj3, proj3, proj3, dt3]

    rb, sb = pl.pallas_call(
        functools.partial(_bstate_kernel, g_chunks=gc),
        grid=(nb, ng),
        in_specs=bwd_specs + [const((1, DT_W)), const((1, DT_W)), const((LANES, SSD_D_INNER))],
        out_specs=[
            pl.BlockSpec((1, gc, RET_QK, RET_V_DIM), lambda b, s: (b, ng - 1 - s, 0, 0)),
            pl.BlockSpec((1, gc, SSD_BC, SSD_GW), lambda b, s: (b, ng - 1 - s, 0, 0)),
        ],
        out_shape=[
            jax.ShapeDtypeStruct((nb, nc, RET_QK, RET_V_DIM), BF16),
            jax.ShapeDtypeStruct((nb, nc, SSD_BC, SSD_GW), BF16),
        ],
        scratch_shapes=[
            pltpu.VMEM((RET_HEADS, RET_QK_DIM, RET_V_DIM), F32),
            pltpu.VMEM((SSD_GROUPS, SSD_STATE, SSD_GW), F32),
        ],
        compiler_params=params,
        name="mix_bwd_state",
    )(*bwd_args, dtbias, alog, e_b)

    fwd_specs, fwd_args = [], []
    for p in range(gc):
        def chunk_of(s, p=p):
            return gc * s + p
        fwd_specs += [chunk_spec(qb, COL_Q, chunk_of), chunk_spec(qb, COL_K, chunk_of),
                      chunk_spec(vb, COL_V, chunk_of), chunk_spec(vb, COL_G, chunk_of),
                      chunk_spec(xw_, COL_Z, chunk_of), chunk_spec(xw_, COL_XS, chunk_of),
                      chunk_spec(bcw_, COL_B, chunk_of),
                      pl.BlockSpec((1, CHUNK, DT_W), lambda b, s, f=chunk_of: (b, _mem_block(f(s), nr), 0))]
        fwd_args += [proj3] * 7 + [dt3]

    outs = pl.pallas_call(
        functools.partial(_main_kernel, g_chunks=gc),
        grid=(nb, ng),
        in_specs=fwd_specs + [
            pl.BlockSpec((1, gc, RET_QK, RET_V_DIM), lambda b, s: (b, s, 0, 0)),
            pl.BlockSpec((1, gc, SSD_BC, SSD_GW), lambda b, s: (b, s, 0, 0)),
            const((1, RET_V)), const((1, DT_W)), const((1, DT_W)),
            const((LANES, SSD_D_INNER)), const((LANES, SSD_HEADS * LANES)),
            const((1, SSD_D_INNER)), const((1, SSD_D_INNER)),
        ],
        out_specs=([pl.BlockSpec((1, 1, CHUNK, RET_V), lambda b, s: (b, s, 0, 0))] * gc
                   + [pl.BlockSpec((1, 1, CHUNK, SSD_D_INNER), lambda b, s: (b, s, 0, 0))] * gc),
        out_shape=([jax.ShapeDtypeStruct((nb, ng, CHUNK, RET_V), BF16)] * gc
                   + [jax.ShapeDtypeStruct((nb, ng, CHUNK, SSD_D_INNER), BF16)] * gc),
        scratch_shapes=[
            pltpu.VMEM((RET_HEADS, RET_QK_DIM, RET_V_DIM), F32),
            pltpu.VMEM((SSD_GROUPS, SSD_STATE, SSD_GW), F32),
            pltpu.VMEM((gc, CHUNK, SSD_D_INNER), F32),
        ],
        compiler_params=params,
        name="mix_main",
    )(*fwd_args, rb, sb, gnw, dtbias, alog, e_f, c_f, dskip, nw)
    return list(outs[:gc]), list(outs[gc:])


def _merge_kernel(*refs, g_chunks):
    h_ref = refs[0]
    yr_refs = refs[1:1 + g_chunks]
    ys_refs = refs[1 + g_chunks:1 + 2 * g_chunks]
    gates_ref, wr_ref, ws_ref, wo_ref, o_ref = refs[1 + 2 * g_chunks:]
    yr_in = jnp.concatenate([r[0, 0] for r in yr_refs], axis=0)
    ys_in = jnp.concatenate([r[0, 0] for r in ys_refs], axis=0)
    yr = _dot(yr_in, wr_ref[...])
    ys = _dot(ys_in, ws_ref[...])
    gr = _sigmoid(gates_ref[0, :, :D_MODEL].astype(F32))
    gs = _sigmoid(gates_ref[0, :, D_MODEL:].astype(F32))
    merged = (gr * yr + gs * ys).astype(BF16)
    o_ref[0] = h_ref[0] + _dot(merged, wo_ref[...])


def _merge_call(hm3, yr_list, ys_list, proj3, wr, ws, wo):
    nb, lm, _ = hm3.shape
    gc = len(yr_list)
    ng = yr_list[0].shape[1]
    tm = gc * CHUNK
    gw = 2 * D_MODEL

    def mixer_specs(arrays, width):
        specs, args = [], []
        for q in range(gc):
            if q < gc - 1:
                specs.append(pl.BlockSpec((1, 1, CHUNK, width), lambda b, t: (b, t, 0, 0)))
            else:
                specs.append(pl.BlockSpec((1, 1, CHUNK, width), lambda b, t: (b, lax.rem(t + 1, ng), 0, 0)))
            args.append(arrays[(q + 1) % gc])
        return specs, args

    yr_specs, yr_args = mixer_specs(yr_list, RET_V)
    ys_specs, ys_args = mixer_specs(ys_list, SSD_D_INNER)

    def resident(shape):
        return pl.BlockSpec(shape, lambda b, t: (0,) * len(shape))

    return pl.pallas_call(
        functools.partial(_merge_kernel, g_chunks=gc),
        grid=(nb, lm // tm),
        in_specs=[pl.BlockSpec((1, tm, D_MODEL), lambda b, t: (b, t, 0))] + yr_specs + ys_specs + [
            pl.BlockSpec((1, tm, gw), lambda b, t: (b, t, COL_GATES // gw)),
            resident((RET_V, D_MODEL)), resident((SSD_D_INNER, D_MODEL)), resident((D_MODEL, D_MODEL)),
        ],
        out_specs=pl.BlockSpec((1, tm, D_MODEL), lambda b, t: (b, t, 0)),
        out_shape=jax.ShapeDtypeStruct((nb, lm, D_MODEL), F32),
        compiler_params=pltpu.CompilerParams(
            dimension_semantics=("parallel", "parallel"), vmem_limit_bytes=VMEM_LIMIT),
        name="merge_out",
    )(hm3, *yr_args, *ys_args, proj3, wr, ws, wo)


def _ffn_kernel(h_ref, hp_ref, hn_ref, nw_ref, wup_ref, cw_ref, cb_ref, wd_ref, fnw_ref, o_ref, u_ref, act_ref,
                *, tm, tn):
    nw = nw_ref[...]
    u_ref[0:HALO, :] = _rms(hp_ref[0], nw).astype(BF16)
    u_ref[HALO:HALO + tm, :] = _rms(h_ref[0], nw).astype(BF16)
    u_ref[HALO + tm:2 * HALO + tm, :] = _rms(hn_ref[0], nw).astype(BF16)
    u = u_ref[...]
    nblk = D_FF // tn
    half = (nblk + 1) // 2

    def up(j):
        return (_dot(u, wup_ref[:, j * tn:(j + 1) * tn]),
                _dot(u, wup_ref[:, D_FF + j * tn:D_FF + (j + 1) * tn]))

    nxt = up(0)
    for j in range(nblk):
        fg, fu = nxt
        if j + 1 < nblk:
            nxt = up(j + 1)
        gc = slice(j * tn, (j + 1) * tn)
        uc = slice(D_FF + j * tn, D_FF + (j + 1) * tn)
        act = _silu(_conv_rows(fg, cw_ref, cb_ref, gc, tm)) * _conv_rows(fu, cw_ref, cb_ref, uc, tm)
        act_ref[:, gc] = act.astype(BF16)
        if j == half - 1:
            y_a = _dot(act_ref[:, :half * tn], wd_ref[:half * tn, :])
    y = h_ref[0] + y_a + _dot(act_ref[:, half * tn:], wd_ref[half * tn:, :])
    o_ref[0] = _rms(y, fnw_ref[...])


def _ffn_call(hmid3, nw, w_up, cw, cb, w_down, fnw, nb, s_len):
    lm = hmid3.shape[1]
    tm = 512 if s_len % 512 == 0 else CHUNK
    tn = 256
    nh = lm // HALO
    hpt = tm // HALO

    def prev_blk(t):
        return jnp.where(t == 0, nh - 1, t * hpt - 1)

    def resident(shape):
        return pl.BlockSpec(shape, lambda b, t: (0,) * len(shape), pipeline_mode=pl.Buffered(1))

    return pl.pallas_call(
        functools.partial(_ffn_kernel, tm=tm, tn=tn),
        grid=(nb, s_len // tm),
        in_specs=[
            pl.BlockSpec((1, tm, D_MODEL), lambda b, t: (b, t, 0)),
            pl.BlockSpec((1, HALO, D_MODEL), lambda b, t: (b, prev_blk(t), 0)),
            pl.BlockSpec((1, HALO, D_MODEL), lambda b, t: (b, (t + 1) * hpt, 0)),
            resident((1, D_MODEL)),
            resident((D_MODEL, 2 * D_FF)),
            resident((3, 2 * D_FF)),
            resident((1, 2 * D_FF)),
            resident((D_FF, D_MODEL)),
            resident((1, D_MODEL)),
        ],
        out_specs=pl.BlockSpec((1, tm, D_MODEL), lambda b, t: (b, t, 0)),
        out_shape=jax.ShapeDtypeStruct((nb, s_len, D_MODEL), F32),
        scratch_shapes=[pltpu.VMEM((tm + 2 * HALO, D_MODEL), BF16), pltpu.VMEM((tm, D_FF), BF16)],
        compiler_params=pltpu.CompilerParams(
            dimension_semantics=("parallel", "parallel"), vmem_limit_bytes=VMEM_LIMIT),
        name="ffn",
    )(hmid3, hmid3, hmid3, nw, w_up, cw, cb, w_down, fnw)


def _position_tables(s_len):
    half = RET_QK_DIM // 2
    inv = ROPE_BASE ** (-jnp.arange(half, dtype=F32) / half)
    pos = jnp.concatenate([N_META + jnp.arange(s_len), jnp.zeros((PAD,), jnp.int32), jnp.arange(N_META)])
    ang = pos.astype(F32)[:, None] * inv[None, :]
    cos, sin = jnp.cos(ang), jnp.sin(ang)
    valid = jnp.concatenate([jnp.ones((s_len,), F32), jnp.zeros((PAD,), F32), jnp.ones((N_META,), F32)])
    return (jnp.concatenate([cos, cos], axis=1), jnp.concatenate([-sin, sin], axis=1),
            jnp.broadcast_to(valid[:, None], (s_len + CHUNK, LANES)))


def kernel(x, meta_tokens, norm_mix_w, w_in, ret_gn_w, w_ret_out, w_ssd_conv, b_ssd_conv, dt_bias_f, dt_bias_b, a_log_f, a_log_b, d_skip, ssd_norm_w, w_ssd_out, w_out, norm_ffn_w, w_ffn_up, w_ffn_conv, b_ffn_conv, w_ffn_down, final_norm_w):
    assert norm_mix_w.shape[0] == 1, "single-layer block"
    nb, s_len, _ = x.shape
    assert s_len % CHUNK == 0
    lm = s_len + CHUNK
    nc = lm // CHUNK

    tail = jnp.concatenate([jnp.zeros((PAD, D_MODEL), x.dtype), meta_tokens.astype(x.dtype)], axis=0)

    wi = w_in[0]
    dt0 = _W_MAIN_COLS
    w_a = wi[:, :dt0].astype(BF16)
    w_g = wi[:, dt0 + 2 * SSD_HEADS:].astype(BF16)
    w_dt = jnp.pad(wi[:, dt0:dt0 + 2 * SSD_HEADS], ((0, 0), (0, DT_W - 2 * SSD_HEADS))).astype(BF16)
    cos, sin, valid = _position_tables(s_len)

    proj3, dt3, hm = _proj_call(x, tail, norm_mix_w, w_a, w_g, w_dt, cos, sin, valid, w_ssd_conv[0], b_ssd_conv)

    lane_pad = jnp.zeros((1, DT_W - 2 * SSD_HEADS), F32)
    dtbias = jnp.concatenate([dt_bias_f, dt_bias_b, lane_pad], axis=1)
    alog = jnp.concatenate([a_log_f, a_log_b, lane_pad], axis=1)
    dskip = jnp.repeat(d_skip, SSD_HEAD_DIM, axis=1)
    y_ret, y_ssd = _mixers(proj3, dt3, ret_gn_w, dtbias, alog, dskip, ssd_norm_w, nb, nc)

    h_mid = _merge_call(hm, y_ret, y_ssd, proj3, w_ret_out[0].astype(BF16), w_ssd_out[0].astype(BF16),
                        w_out[0].astype(BF16))

    return _ffn_call(h_mid, norm_ffn_w, w_ffn_up[0].astype(BF16), w_ffn_conv[0],
                     b_ffn_conv, w_ffn_down[0].astype(BF16), final_norm_w.reshape(1, D_MODEL), nb, s_len)
```

```python
import functools
import math

import jax
import jax.numpy as jnp
import numpy as np
from jax import lax
from jax.experimental import pallas as pl
from jax.experimental.pallas import tpu as pltpu

F32 = jnp.float32
BF16 = jnp.bfloat16

D_MODEL = 1024
N_META = 16
CHUNK = 128
PAD = CHUNK - N_META
RET_HEADS = 4
RET_QK_DIM = 128
RET_V_DIM = 256
RET_QK = RET_HEADS * RET_QK_DIM
RET_V = RET_HEADS * RET_V_DIM
SSD_D_INNER = 2 * D_MODEL
SSD_HEAD_DIM = 64
SSD_HEADS = SSD_D_INNER // SSD_HEAD_DIM
SSD_GROUPS = 4
SSD_HPG = SSD_HEADS // SSD_GROUPS
SSD_STATE = 128
SSD_GW = SSD_HPG * SSD_HEAD_DIM
SSD_BC = SSD_GROUPS * SSD_STATE
SSD_XBC = SSD_D_INNER + 2 * SSD_BC
D_FF = 2816
EPS = 1e-6
ROPE_BASE = 10000.0
LOG_GAMMA = tuple(math.log(1.0 - 2.0 ** (-5.0 - h)) for h in range(RET_HEADS))
LOG2E = math.log2(math.e)

COL_Z = 0
COL_XS = 2048
COL_B = 4096
COL_Q = 5120
COL_K = 5632
COL_V = 6144
COL_G = 7168
COL_GATES = 8192
PROJ_W = 10240
PROJ_TN = 1024
DT_W = 128

HALO = 16
LANES = 128
VMEM_LIMIT = 56 * 1024 * 1024


def _sigmoid(x):
    return 0.5 + 0.5 * jnp.tanh(0.5 * x)


def _silu(x):
    hx = 0.5 * x
    return hx + hx * jnp.tanh(hx)


def _softplus(x):
    return jnp.maximum(x, 0.0) + jnp.log1p(jnp.exp(-jnp.abs(x)))


def _split3(x):
    hi = x.astype(BF16)
    r1 = x - hi.astype(F32)
    mid = r1.astype(BF16)
    lo = (r1 - mid.astype(F32)).astype(BF16)
    return hi, mid, lo


def _dot(a, b):
    return jnp.dot(a, b, preferred_element_type=F32)


def _dot_tn(a, b):
    return lax.dot_general(a, b, (((0,), (0,)), ((), ())), preferred_element_type=F32)


def _dot_nt(a, b):
    return lax.dot_general(a, b, (((1,), (1,)), ((), ())), preferred_element_type=F32)


def _row_tile(rows, cap):
    best = CHUNK
    t = CHUNK
    while t <= min(rows, cap):
        if rows % t == 0:
            best = t
        t += CHUNK
    return best


def _rms(x, w):
    ms = jnp.mean(x * x, axis=-1, keepdims=True)
    return x * lax.rsqrt(ms + EPS) * w


def _conv_rows(f, w_ref, b_ref, cols, tm):
    prev = pltpu.roll(f, 1, 0)[HALO:HALO + tm]
    nxt = pltpu.roll(f, f.shape[0] - 1, 0)[HALO:HALO + tm]
    return (w_ref[0:1, cols] * prev + w_ref[1:2, cols] * f[HALO:HALO + tm] + w_ref[2:3, cols] * nxt
            + b_ref[:, cols])


_W_MAIN_COLS = 2 * RET_QK + 2 * RET_V + SSD_D_INNER + SSD_XBC


def _weight_source(c0):
    if c0 < COL_Q:
        return 0, c0 + 2 * RET_QK + 2 * RET_V
    if c0 < COL_GATES:
        return 0, c0 - COL_Q
    return 1, c0 - COL_GATES


def _proj_kernel(*refs, tm, n_tail_x, s_len):
    x_ref = refs[0]
    xc_refs = refs[1:1 + n_tail_x]
    (tail_ref, xp_ref, xn_ref, nw_ref, wa_ref, wg_ref, wdt_ref, cos_ref, sin_ref, valid_ref, cw_ref, cb_ref,
     o_ref, odt_ref, hm_ref, u_ref) = refs[1 + n_tail_x:]
    t = pl.program_id(1)
    is_last = t == pl.num_programs(1) - 1
    nw = nw_ref[...]

    tail = tail_ref[...]
    last_rows = jnp.concatenate([xc_ref[0] for xc_ref in xc_refs] + [tail], axis=0)
    rows = jnp.where(is_last, last_rows, x_ref[0])
    hm_ref[0] = rows
    prev = jnp.where(t == 0, tail[CHUNK - HALO:CHUNK, :], xp_ref[0])
    nxt = jnp.where((t + 1) * tm == s_len, tail[0:HALO, :], xn_ref[0])
    u_ref[0:HALO, :] = _rms(prev, nw).astype(BF16)
    u_ref[HALO:HALO + tm, :] = _rms(rows, nw).astype(BF16)
    u_ref[HALO + tm:2 * HALO + tm, :] = _rms(nxt, nw).astype(BF16)

    u = u_ref[HALO:HALO + tm, :]
    odt_ref[0] = _dot(u, wdt_ref[...])
    cos = cos_ref[...]
    sin = sin_ref[...]
    valid = valid_ref[...]
    w_refs = (wa_ref, wg_ref)
    for c0 in range(0, PROJ_W, PROJ_TN):
        src, s0 = _weight_source(c0)
        w = w_refs[src][:, s0:s0 + PROJ_TN]
        if COL_XS <= c0 < COL_Q:
            f = _dot(u_ref[...], w)
            x0 = c0 - COL_XS
            act = _silu(_conv_rows(f, cw_ref, cb_ref, slice(x0, x0 + PROJ_TN), tm))
            for k in range(PROJ_TN // LANES):
                o_ref[0, :, c0 + k * LANES:c0 + (k + 1) * LANES] = (
                    act[:, k * LANES:(k + 1) * LANES] * valid).astype(BF16)
        elif COL_Q <= c0 < COL_V:
            acc = _dot(u, w)
            for hb in range(PROJ_TN // LANES):
                a = acc[:, hb * LANES:(hb + 1) * LANES]
                r = a * cos + pltpu.roll(a, RET_QK_DIM // 2, 1) * sin
                if c0 + hb * LANES >= COL_K:
                    r = r * (RET_QK_DIM ** -0.5)
                o_ref[0, :, c0 + hb * LANES:c0 + (hb + 1) * LANES] = r.astype(BF16)
        else:
            o_ref[0, :, c0:c0 + PROJ_TN] = _dot(u, w).astype(BF16)


def _proj_call(x, tail, nw, wa, wg, wdt, cos, sin, valid, cw, cb):
    nb, s_len, _ = x.shape
    lm = s_len + CHUNK
    tm = _row_tile(lm, 384)
    nt = lm // tm
    hpt = tm // HALO
    n_tail_x = tm // CHUNK - 1
    n_xh = s_len // HALO

    def resident(shape):
        return pl.BlockSpec(shape, lambda b, t: (0,) * len(shape), pipeline_mode=pl.Buffered(1))

    tail_x_specs = [
        pl.BlockSpec((1, CHUNK, D_MODEL), lambda b, t, i=i: (b, s_len // CHUNK - n_tail_x + i, 0))
        for i in range(n_tail_x)]

    return pl.pallas_call(
        functools.partial(_proj_kernel, tm=tm, n_tail_x=n_tail_x, s_len=s_len),
        grid=(nb, nt),
        in_specs=[pl.BlockSpec((1, tm, D_MODEL), lambda b, t: (b, jnp.minimum(t, nt - 2), 0))] + tail_x_specs + [
            resident((CHUNK, D_MODEL)),
            pl.BlockSpec((1, HALO, D_MODEL), lambda b, t: (b, jnp.maximum(t * hpt - 1, 0), 0)),
            pl.BlockSpec((1, HALO, D_MODEL),
                         lambda b, t: (b, jnp.where(t == nt - 1, 0, jnp.minimum((t + 1) * hpt, n_xh - 1)), 0)),
            resident((1, D_MODEL)),
            resident((D_MODEL, _W_MAIN_COLS)),
            resident((D_MODEL, 2 * D_MODEL)),
            resident((D_MODEL, DT_W)),
            pl.BlockSpec((tm, LANES), lambda b, t: (t, 0)),
            pl.BlockSpec((tm, LANES), lambda b, t: (t, 0)),
            pl.BlockSpec((tm, LANES), lambda b, t: (t, 0)),
            resident((3, SSD_XBC)),
            resident((1, SSD_XBC)),
        ],
        out_specs=[
            pl.BlockSpec((1, tm, PROJ_W), lambda b, t: (b, t, 0)),
            pl.BlockSpec((1, tm, DT_W), lambda b, t: (b, t, 0)),
            pl.BlockSpec((1, tm, D_MODEL), lambda b, t: (b, t, 0)),
        ],
        out_shape=[
            jax.ShapeDtypeStruct((nb, lm, PROJ_W), BF16),
            jax.ShapeDtypeStruct((nb, lm, DT_W), F32),
            jax.ShapeDtypeStruct((nb, lm, D_MODEL), F32),
        ],
        scratch_shapes=[pltpu.VMEM((tm + 2 * HALO, D_MODEL), BF16)],
        compiler_params=pltpu.CompilerParams(
            dimension_semantics=("parallel", "parallel"), vmem_limit_bytes=VMEM_LIMIT),
        name="in_proj",
    )(x, *([x] * n_tail_x), tail, x, x, nw, wa, wg, wdt, cos, sin, valid, cw, cb)


def _chunk_pos():
    return lax.broadcasted_iota(jnp.int32, (CHUNK, 1), 0).astype(F32)


def _ret_bstate_chunk(k_ref, v_ref, o_ref, r_ref):
    pos = _chunk_pos()
    for h in range(RET_HEADS):
        lg = LOG_GAMMA[h]
        r = r_ref[h]
        o_ref[h * RET_QK_DIM:(h + 1) * RET_QK_DIM, :] = r.astype(BF16)
        k = k_ref[:, h * RET_QK_DIM:(h + 1) * RET_QK_DIM].astype(F32)
        kd = (k * jnp.exp(lg * pos)).astype(BF16)
        v = v_ref[:, h * RET_V_DIM:(h + 1) * RET_V_DIM]
        r_ref[h] = math.exp(CHUNK * lg) * r + _dot_tn(kd, v)


def _ret_main_chunk(q_ref, k_ref, v_ref, g_ref, rb_ref, gnw_ref, o_ref, r_ref):
    pos = _chunk_pos()
    li = lax.broadcasted_iota(jnp.int32, (CHUNK, CHUNK), 0)
    si = lax.broadcasted_iota(jnp.int32, (CHUNK, CHUNK), 1)
    dist = jnp.abs(li - si).astype(F32)
    for h in range(RET_HEADS):
        lg = LOG_GAMMA[h]
        q = q_ref[:, h * RET_QK_DIM:(h + 1) * RET_QK_DIM]
        k = k_ref[:, h * RET_QK_DIM:(h + 1) * RET_QK_DIM]
        v = v_ref[:, h * RET_V_DIM:(h + 1) * RET_V_DIM]
        r = r_ref[h]
        s = _dot_nt(q, k) * jnp.exp(lg * dist)
        y = _dot(s.astype(BF16), v)
        y = y + jnp.exp(lg * (pos + 1.0)) * _dot(q, r.astype(BF16))
        rb = rb_ref[h * RET_QK_DIM:(h + 1) * RET_QK_DIM, :]
        y = y + jnp.exp(lg * (CHUNK - pos)) * _dot(q, rb)
        kd = (k.astype(F32) * jnp.exp(lg * (CHUNK - 1.0 - pos))).astype(BF16)
        r_ref[h] = math.exp(CHUNK * lg) * r + _dot_tn(kd, v)
        mu = jnp.mean(y, axis=-1, keepdims=True)
        yc = y - mu
        var = jnp.mean(yc * yc, axis=-1, keepdims=True)
        yn = yc * lax.rsqrt(var + EPS)
        sl = slice(h * RET_V_DIM, (h + 1) * RET_V_DIM)
        g = g_ref[:, sl].astype(F32)
        o_ref[:, sl] = (_silu(g) * (yn * gnw_ref[:, sl])).astype(BF16)


def _mem_block(c, nr):
    return jnp.where(c == 0, nr, c - 1)


def _valid_rows(is_meta_chunk):
    row = lax.broadcasted_iota(jnp.int32, (CHUNK, 1), 0)
    return jnp.where(jnp.logical_and(is_meta_chunk, row < PAD), 0.0, 1.0).astype(F32)


def _ssd_decays(dt_ref, dtb_ref, alog_ref, valid):
    dt = _softplus(dt_ref[...] + dtb_ref[...]) * valid
    da = dt * (-LOG2E * jnp.exp(alog_ref[...]))
    li = lax.broadcasted_iota(jnp.int32, (CHUNK, CHUNK), 0)
    si = lax.broadcasted_iota(jnp.int32, (CHUNK, CHUNK), 1)
    tri = jnp.where(si <= li, 1.0, 0.0).astype(BF16)
    hi, mid, lo = _split3(da)
    acs = _dot(tri, hi) + _dot(tri, mid) + _dot(tri, lo)
    return dt, da, acs


def _pack_hi_lo(x):
    lane = lax.broadcasted_iota(jnp.int32, x.shape, 1)
    hi = x.astype(BF16).astype(F32)
    return jnp.where(lane < LANES // 2, hi, pltpu.roll(x - hi, LANES // 2, 1)).astype(BF16)


def _ssd_bstate_chunk(xs_ref, b_ref, dt_ref, dtb_ref, alog_ref, eb_ref, o_ref, s_ref, valid):
    dt, da, acs = _ssd_decays(dt_ref, dtb_ref, alog_ref, valid)
    stack = jnp.concatenate([dt * jnp.exp2(acs - da), jnp.exp2(acs[CHUNK - HALO:CHUNK, :])], axis=0)
    ex = _dot(_pack_hi_lo(stack), eb_ref[...])
    xw = (xs_ref[...].astype(F32) * ex[0:CHUNK, :]).astype(BF16)
    cdec = ex[CHUNK + HALO - 1:CHUNK + HALO, :]
    bm = b_ref[...]
    for g in range(SSD_GROUPS):
        sl = slice(g * SSD_GW, (g + 1) * SSD_GW)
        s = s_ref[g]
        o_ref[g * SSD_STATE:(g + 1) * SSD_STATE, :] = s.astype(BF16)
        s_ref[g] = cdec[:, sl] * s + _dot_tn(bm[:, g * SSD_STATE:(g + 1) * SSD_STATE], xw[:, sl])


def _pack3(x, src):
    lane = lax.broadcasted_iota(jnp.int32, x.shape, 1)
    hi = x.astype(BF16).astype(F32)
    r1 = x - hi
    mid = r1.astype(BF16).astype(F32)
    lo = r1 - mid
    packed = jnp.where(lane < src + SSD_HEADS, hi,
                       jnp.where(lane < src + 2 * SSD_HEADS, pltpu.roll(mid, SSD_HEADS, 1),
                                 pltpu.roll(lo, 2 * SSD_HEADS, 1)))
    return packed.astype(BF16)


def _ssd_main_chunk(z_ref, xs_ref, bc_ref, dt_ref, sb_ref, dtb_ref, alog_ref, ef_ref, cf_ref,
                    dskip_ref, nw_ref, o_ref, s_ref, valid):
    dt, da, acs = _ssd_decays(dt_ref, dtb_ref, alog_ref, valid)
    last = acs[CHUNK - 1:CHUNK, :]
    to_right = last - (acs - da)
    ldt_t = jnp.log2(dt).T
    row_f = acs.T - ldt_t
    row_b = ldt_t - to_right.T
    diag_t = jnp.log2(dt + pltpu.roll(dt, LANES - SSD_HEADS, 1)).T

    li = lax.broadcasted_iota(jnp.int32, (CHUNK, CHUNK), 0)
    si = lax.broadcasted_iota(jnp.int32, (CHUNK, CHUNK), 1)
    below = si < li
    above = si > li
    lane = lax.broadcasted_iota(jnp.int32, (CHUNK, LANES), 1)
    first_half = lane < SSD_HEAD_DIM
    zero = jnp.zeros((CHUNK, LANES), BF16)
    acs_parts = _pack3(acs, 0)
    src_w = _pack_hi_lo(dt * jnp.exp2(last - acs))
    tiles = SSD_GW // LANES

    for g in range(SSD_GROUPS):
        sl = slice(g * SSD_GW, (g + 1) * SSD_GW)
        bg = bc_ref[:, g * SSD_STATE:(g + 1) * SSD_STATE]
        cg = bc_ref[:, SSD_BC + g * SSD_STATE:SSD_BC + (g + 1) * SSD_STATE]
        cb = _dot_nt(cg, bg)
        col_f = _dot(acs_parts, cf_ref[:, g * SSD_HPG * LANES:(g + 1) * SSD_HPG * LANES])

        def mix(h, causal_col, anti_col, cb=cb):
            lb = SSD_HEADS + h
            causal = causal_col - row_f[h:h + 1, :]
            anti = row_b[lb:lb + 1, :] + anti_col
            e = jnp.where(below, causal, jnp.where(above, anti, diag_t[h:h + 1, :]))
            return (cb * jnp.exp2(e)).astype(BF16)

        ea_f, ea_b, y_mix = [], [], []
        for pp in range(tiles):
            ha = g * SSD_HPG + 2 * pp
            hb = ha + 1
            fa = col_f[:, 2 * pp * LANES:(2 * pp + 1) * LANES]
            fb = col_f[:, (2 * pp + 1) * LANES:(2 * pp + 2) * LANES]
            ra = jnp.broadcast_to(to_right[:, SSD_HEADS + ha:SSD_HEADS + ha + 1], (CHUNK, LANES))
            rb = jnp.broadcast_to(to_right[:, SSD_HEADS + hb:SSD_HEADS + hb + 1], (CHUNK, LANES))
            ea_f.append(jnp.exp2(jnp.where(first_half, fa, fb)))
            ea_b.append(jnp.exp2(jnp.where(first_half, ra, rb)))
            xp = xs_ref[:, (g * tiles + pp) * LANES:(g * tiles + pp + 1) * LANES]
            rhs = jnp.concatenate([jnp.where(first_half, xp, zero), jnp.where(first_half, zero, xp)], axis=0)
            y_mix.append(_dot(jnp.concatenate([mix(ha, fa, ra), mix(hb, fb, rb)], axis=1), rhs))
        ea_fg = jnp.concatenate(ea_f, axis=1)
        ea_bg = jnp.concatenate(ea_b, axis=1)

        xs = xs_ref[:, sl].astype(F32)
        xw = (xs * _dot(src_w, ef_ref[:, sl])).astype(BF16)
        s = s_ref[g]
        y_off = ea_fg * _dot(cg, s.astype(BF16))
        y_off = y_off + ea_bg * _dot(cg, sb_ref[g * SSD_STATE:(g + 1) * SSD_STATE, :])
        s_ref[g] = ea_fg[CHUNK - 1:CHUNK, :] * s + _dot_tn(bg, xw)
        y = jnp.concatenate(y_mix, axis=1) + y_off + xs * dskip_ref[:, sl]
        y = y * _silu(z_ref[:, sl].astype(F32))
        o_ref[:, sl] = _rms(y, nw_ref[:, sl]).astype(BF16)


def _head_expander(offset):
    e = np.zeros((LANES, SSD_D_INNER), np.float32)
    for h in range(SSD_HEADS):
        e[offset + h, h * SSD_HEAD_DIM:(h + 1) * SSD_HEAD_DIM] = 1.0
        e[LANES // 2 + offset + h, h * SSD_HEAD_DIM:(h + 1) * SSD_HEAD_DIM] = 1.0
    return jnp.asarray(e, BF16)


def _lane_broadcaster(src):
    e = np.zeros((LANES, SSD_HEADS * LANES), np.float32)
    for h in range(SSD_HEADS):
        for part in range(3):
            e[src + part * SSD_HEADS + h, h * LANES:(h + 1) * LANES] = 1.0
    return jnp.asarray(e, BF16)


def _bstate_kernel(*refs, g_chunks):
    n_in = 5 * g_chunks
    dtb_ref, alog_ref, eb_ref, rb_ref, sb_ref, r_ref, s_ref = refs[n_in:]
    step = pl.program_id(1)

    @pl.when(step == 0)
    def _():
        r_ref[...] = jnp.zeros_like(r_ref)
        s_ref[...] = jnp.zeros_like(s_ref)

    is_last = step == pl.num_programs(1) - 1
    for p in range(g_chunks):
        k_ref, v_ref, xs_ref, b_ref, dt_ref = refs[5 * p:5 * p + 5]
        slot = g_chunks - 1 - p
        _ret_bstate_chunk(k_ref.at[0], v_ref.at[0], rb_ref.at[0, slot], r_ref)
        valid = _valid_rows(is_last if slot == 0 else False)
        _ssd_bstate_chunk(xs_ref.at[0], b_ref.at[0], dt_ref.at[0], dtb_ref, alog_ref, eb_ref,
                          sb_ref.at[0, slot], s_ref, valid)


def _main_kernel(*refs, g_chunks):
    n_in = 8 * g_chunks
    rb_ref, sb_ref, gnw_ref, dtb_ref, alog_ref, ef_ref, cf_ref, dskip_ref, nw_ref = refs[n_in:n_in + 9]
    yr_refs = refs[n_in + 9:n_in + 9 + g_chunks]
    ys_refs = refs[n_in + 9 + g_chunks:n_in + 9 + 2 * g_chunks]
    r_ref, s_ref = refs[n_in + 9 + 2 * g_chunks:]
    step = pl.program_id(1)

    @pl.when(step == 0)
    def _():
        r_ref[...] = jnp.zeros_like(r_ref)
        s_ref[...] = jnp.zeros_like(s_ref)

    for p in range(g_chunks):
        q_ref, k_ref, v_ref, g_ref, z_ref, xs_ref, bc_ref, dt_ref = refs[8 * p:8 * p + 8]
        _ret_main_chunk(q_ref.at[0], k_ref.at[0], v_ref.at[0], g_ref.at[0], rb_ref.at[0, p], gnw_ref,
                        yr_refs[p].at[0, 0], r_ref)
        valid = _valid_rows(step == 0 if p == 0 else False)
        _ssd_main_chunk(z_ref.at[0], xs_ref.at[0], bc_ref.at[0], dt_ref.at[0], sb_ref.at[0, p], dtb_ref, alog_ref,
                        ef_ref, cf_ref, dskip_ref, nw_ref, ys_refs[p].at[0, 0], s_ref, valid)


def _chunks_per_step(nc):
    return 3 if nc % 3 == 0 else 1


def _mixers(proj3, dt3, gnw, dtbias, alog, dskip, nw, nb, nc):
    nr = nc - 1
    gc = _chunks_per_step(nc)
    ng = nc // gc
    qb, vb = RET_QK, RET_V
    xw_, bw_, bcw_ = SSD_D_INNER, SSD_BC, 2 * SSD_BC
    e_f = _head_expander(0)
    e_b = _head_expander(SSD_HEADS)
    c_f = _lane_broadcaster(0)
    params = pltpu.CompilerParams(dimension_semantics=("parallel", "arbitrary"), vmem_limit_bytes=VMEM_LIMIT)

    def const(shape):
        return pl.BlockSpec(shape, lambda b, s: (0,) * len(shape))

    def chunk_spec(width, col, chunk_of):
        return pl.BlockSpec((1, CHUNK, width), lambda b, s: (b, _mem_block(chunk_of(s), nr), col // width))

    bwd_specs, bwd_args = [], []
    for p in range(gc):
        def chunk_of(s, p=p):
            return nc - 1 - (gc * s + p)
        bwd_specs += [chunk_spec(qb, COL_K, chunk_of), chunk_spec(vb, COL_V, chunk_of),
                      chunk_spec(xw_, COL_XS, chunk_of), chunk_spec(bw_, COL_B, chunk_of),
                      pl.BlockSpec((1, CHUNK, DT_W), lambda b, s, f=chunk_of: (b, _mem_block(f(s), nr), 0))]
        bwd_args += [proj3, proj3, proj3, proj3, dt3]

    rb, sb = pl.pallas_call(
        functools.partial(_bstate_kernel, g_chunks=gc),
        grid=(nb, ng),
        in_specs=bwd_specs + [const((1, DT_W)), const((1, DT_W)), const((LANES, SSD_D_INNER))],
        out_specs=[
            pl.BlockSpec((1, gc, RET_QK, RET_V_DIM), lambda b, s: (b, ng - 1 - s, 0, 0)),
            pl.BlockSpec((1, gc, SSD_BC, SSD_GW), lambda b, s: (b, ng - 1 - s, 0, 0)),
        ],
        out_shape=[
            jax.ShapeDtypeStruct((nb, nc, RET_QK, RET_V_DIM), BF16),
            jax.ShapeDtypeStruct((nb, nc, SSD_BC, SSD_GW), BF16),
        ],
        scratch_shapes=[
            pltpu.VMEM((RET_HEADS, RET_QK_DIM, RET_V_DIM), F32),
            pltpu.VMEM((SSD_GROUPS, SSD_STATE, SSD_GW), F32),
        ],
        compiler_params=params,
        name="mix_bwd_state",
    )(*bwd_args, dtbias, alog, e_b)

    fwd_specs, fwd_args = [], []
    for p in range(gc):
        def chunk_of(s, p=p):
            return gc * s + p
        fwd_specs += [chunk_spec(qb, COL_Q, chunk_of), chunk_spec(qb, COL_K, chunk_of),
                      chunk_spec(vb, COL_V, chunk_of), chunk_spec(vb, COL_G, chunk_of),
                      chunk_spec(xw_, COL_Z, chunk_of), chunk_spec(xw_, COL_XS, chunk_of),
                      chunk_spec(bcw_, COL_B, chunk_of),
                      pl.BlockSpec((1, CHUNK, DT_W), lambda b, s, f=chunk_of: (b, _mem_block(f(s), nr), 0))]
        fwd_args += [proj3] * 7 + [dt3]

    outs = pl.pallas_call(
        functools.partial(_main_kernel, g_chunks=gc),
        grid=(nb, ng),
        in_specs=fwd_specs + [
            pl.BlockSpec((1, gc, RET_QK, RET_V_DIM), lambda b, s: (b, s, 0, 0)),
            pl.BlockSpec((1, gc, SSD_BC, SSD_GW), lambda b, s: (b, s, 0, 0)),
            const((1, RET_V)), const((1, DT_W)), const((1, DT_W)),
            const((LANES, SSD_D_INNER)), const((LANES, SSD_HEADS * LANES)),
            const((1, SSD_D_INNER)), const((1, SSD_D_INNER)),
        ],
        out_specs=([pl.BlockSpec((1, 1, CHUNK, RET_V), lambda b, s: (b, s, 0, 0))] * gc
                   + [pl.BlockSpec((1, 1, CHUNK, SSD_D_INNER), lambda b, s: (b, s, 0, 0))] * gc),
        out_shape=([jax.ShapeDtypeStruct((nb, ng, CHUNK, RET_V), BF16)] * gc
                   + [jax.ShapeDtypeStruct((nb, ng, CHUNK, SSD_D_INNER), BF16)] * gc),
        scratch_shapes=[
            pltpu.VMEM((RET_HEADS, RET_QK_DIM, RET_V_DIM), F32),
            pltpu.VMEM((SSD_GROUPS, SSD_STATE, SSD_GW), F32),
        ],
        compiler_params=params,
        name="mix_main",
    )(*fwd_args, rb, sb, gnw, dtbias, alog, e_f, c_f, dskip, nw)
    return list(outs[:gc]), list(outs[gc:])


def _merge_kernel(*refs, g_chunks):
    h_ref = refs[0]
    yr_refs = refs[1:1 + g_chunks]
    ys_refs = refs[1 + g_chunks:1 + 2 * g_chunks]
    gates_ref, wr_ref, ws_ref, wo_ref, o_ref = refs[1 + 2 * g_chunks:]
    yr_in = jnp.concatenate([r[0, 0] for r in yr_refs], axis=0)
    ys_in = jnp.concatenate([r[0, 0] for r in ys_refs], axis=0)
    yr = _dot(yr_in, wr_ref[...])
    ys = _dot(ys_in, ws_ref[...])
    gr = _sigmoid(gates_ref[0, :, :D_MODEL].astype(F32))
    gs = _sigmoid(gates_ref[0, :, D_MODEL:].astype(F32))
    merged = (gr * yr + gs * ys).astype(BF16)
    o_ref[0] = h_ref[0] + _dot(merged, wo_ref[...])


def _merge_call(hm3, yr_list, ys_list, proj3, wr, ws, wo):
    nb, lm, _ = hm3.shape
    gc = len(yr_list)
    ng = yr_list[0].shape[1]
    tm = gc * CHUNK
    gw = 2 * D_MODEL

    def mixer_specs(arrays, width):
        specs, args = [], []
        for q in range(gc):
            if q < gc - 1:
                specs.append(pl.BlockSpec((1, 1, CHUNK, width), lambda b, t: (b, t, 0, 0)))
            else:
                specs.append(pl.BlockSpec((1, 1, CHUNK, width), lambda b, t: (b, lax.rem(t + 1, ng), 0, 0)))
            args.append(arrays[(q + 1) % gc])
        return specs, args

    yr_specs, yr_args = mixer_specs(yr_list, RET_V)
    ys_specs, ys_args = mixer_specs(ys_list, SSD_D_INNER)

    def resident(shape):
        return pl.BlockSpec(shape, lambda b, t: (0,) * len(shape))

    return pl.pallas_call(
        functools.partial(_merge_kernel, g_chunks=gc),
        grid=(nb, lm // tm),
        in_specs=[pl.BlockSpec((1, tm, D_MODEL), lambda b, t: (b, t, 0))] + yr_specs + ys_specs + [
            pl.BlockSpec((1, tm, gw), lambda b, t: (b, t, COL_GATES // gw)),
            resident((RET_V, D_MODEL)), resident((SSD_D_INNER, D_MODEL)), resident((D_MODEL, D_MODEL)),
        ],
        out_specs=pl.BlockSpec((1, tm, D_MODEL), lambda b, t: (b, t, 0)),
        out_shape=jax.ShapeDtypeStruct((nb, lm, D_MODEL), F32),
        compiler_params=pltpu.CompilerParams(
            dimension_semantics=("parallel", "parallel"), vmem_limit_bytes=VMEM_LIMIT),
        name="merge_out",
    )(hm3, *yr_args, *ys_args, proj3, wr, ws, wo)


def _ffn_kernel(h_ref, hp_ref, hn_ref, nw_ref, wup_ref, cw_ref, cb_ref, wd_ref, fnw_ref, o_ref, u_ref, act_ref,
                *, tm, tn):
    nw = nw_ref[...]
    u_ref[0:HALO, :] = _rms(hp_ref[0], nw).astype(BF16)
    u_ref[HALO:HALO + tm, :] = _rms(h_ref[0], nw).astype(BF16)
    u_ref[HALO + tm:2 * HALO + tm, :] = _rms(hn_ref[0], nw).astype(BF16)
    u = u_ref[...]
    nblk = D_FF // tn
    half = (nblk + 1) // 2

    def up(j):
        return (_dot(u, wup_ref[:, j * tn:(j + 1) * tn]),
                _dot(u, wup_ref[:, D_FF + j * tn:D_FF + (j + 1) * tn]))

    nxt = up(0)
    for j in range(nblk):
        fg, fu = nxt
        if j + 1 < nblk:
            nxt = up(j + 1)
        gc = slice(j * tn, (j + 1) * tn)
        uc = slice(D_FF + j * tn, D_FF + (j + 1) * tn)
        act = _silu(_conv_rows(fg, cw_ref, cb_ref, gc, tm)) * _conv_rows(fu, cw_ref, cb_ref, uc, tm)
        act_ref[:, gc] = act.astype(BF16)
        if j == half - 1:
            y_a = _dot(act_ref[:, :half * tn], wd_ref[:half * tn, :])
    y = h_ref[0] + y_a + _dot(act_ref[:, half * tn:], wd_ref[half * tn:, :])
    o_ref[0] = _rms(y, fnw_ref[...])


def _ffn_call(hmid3, nw, w_up, cw, cb, w_down, fnw, nb, s_len):
    lm = hmid3.shape[1]
    tm = 512 if s_len % 512 == 0 else CHUNK
    tn = 256
    nh = lm // HALO
    hpt = tm // HALO

    def prev_blk(t):
        return jnp.where(t == 0, nh - 1, t * hpt - 1)

    def resident(shape):
        return pl.BlockSpec(shape, lambda b, t: (0,) * len(shape), pipeline_mode=pl.Buffered(1))

    return pl.pallas_call(
        functools.partial(_ffn_kernel, tm=tm, tn=tn),
        grid=(nb, s_len // tm),
        in_specs=[
            pl.BlockSpec((1, tm, D_MODEL), lambda b, t: (b, t, 0)),
            pl.BlockSpec((1, HALO, D_MODEL), lambda b, t: (b, prev_blk(t), 0)),
            pl.BlockSpec((1, HALO, D_MODEL), lambda b, t: (b, (t + 1) * hpt, 0)),
            resident((1, D_MODEL)),
            resident((D_MODEL, 2 * D_FF)),
            resident((3, 2 * D_FF)),
            resident((1, 2 * D_FF)),
            resident((D_FF, D_MODEL)),
            resident((1, D_MODEL)),
        ],
        out_specs=pl.BlockSpec((1, tm, D_MODEL), lambda b, t: (b, t, 0)),
        out_shape=jax.ShapeDtypeStruct((nb, s_len, D_MODEL), F32),
        scratch_shapes=[pltpu.VMEM((tm + 2 * HALO, D_MODEL), BF16), pltpu.VMEM((tm, D_FF), BF16)],
        compiler_params=pltpu.CompilerParams(
            dimension_semantics=("parallel", "parallel"), vmem_limit_bytes=VMEM_LIMIT),
        name="ffn",
    )(hmid3, hmid3, hmid3, nw, w_up, cw, cb, w_down, fnw)


def _position_tables(s_len):
    half = RET_QK_DIM // 2
    inv = ROPE_BASE ** (-jnp.arange(half, dtype=F32) / half)
    pos = jnp.concatenate([N_META + jnp.arange(s_len), jnp.zeros((PAD,), jnp.int32), jnp.arange(N_META)])
    ang = pos.astype(F32)[:, None] * inv[None, :]
    cos, sin = jnp.cos(ang), jnp.sin(ang)
    valid = jnp.concatenate([jnp.ones((s_len,), F32), jnp.zeros((PAD,), F32), jnp.ones((N_META,), F32)])
    return (jnp.concatenate([cos, cos], axis=1), jnp.concatenate([-sin, sin], axis=1),
            jnp.broadcast_to(valid[:, None], (s_len + CHUNK, LANES)))


def kernel(x, meta_tokens, norm_mix_w, w_in, ret_gn_w, w_ret_out, w_ssd_conv, b_ssd_conv, dt_bias_f, dt_bias_b, a_log_f, a_log_b, d_skip, ssd_norm_w, w_ssd_out, w_out, norm_ffn_w, w_ffn_up, w_ffn_conv, b_ffn_conv, w_ffn_down, final_norm_w):
    assert norm_mix_w.shape[0] == 1, "single-layer block"
    nb, s_len, _ = x.shape
    assert s_len % CHUNK == 0
    lm = s_len + CHUNK
    nc = lm // CHUNK

    tail = jnp.concatenate([jnp.zeros((PAD, D_MODEL), x.dtype), meta_tokens.astype(x.dtype)], axis=0)

    wi = w_in[0]
    dt0 = _W_MAIN_COLS
    w_a = wi[:, :dt0].astype(BF16)
    w_g = wi[:, dt0 + 2 * SSD_HEADS:].astype(BF16)
    w_dt = jnp.pad(wi[:, dt0:dt0 + 2 * SSD_HEADS], ((0, 0), (0, DT_W - 2 * SSD_HEADS))).astype(BF16)
    cos, sin, valid = _position_tables(s_len)

    proj3, dt3, hm = _proj_call(x, tail, norm_mix_w, w_a, w_g, w_dt, cos, sin, valid, w_ssd_conv[0], b_ssd_conv)

    lane_pad = jnp.zeros((1, DT_W - 2 * SSD_HEADS), F32)
    dtbias = jnp.concatenate([dt_bias_f, dt_bias_b, lane_pad], axis=1)
    alog = jnp.concatenate([a_log_f, a_log_b, lane_pad], axis=1)
    dskip = jnp.repeat(d_skip, SSD_HEAD_DIM, axis=1)
    y_ret, y_ssd = _mixers(proj3, dt3, ret_gn_w, dtbias, alog, dskip, ssd_norm_w, nb, nc)

    h_mid = _merge_call(hm, y_ret, y_ssd, proj3, w_ret_out[0].astype(BF16), w_ssd_out[0].astype(BF16),
                        w_out[0].astype(BF16))

    return _ffn_call(h_mid, norm_ffn_w, w_ffn_up[0].astype(BF16), w_ffn_conv[0],
                     b_ffn_conv, w_ffn_down[0].astype(BF16), final_norm_w.reshape(1, D_MODEL), nb, s_len)
```

```python
import functools
import math

import jax
import jax.numpy as jnp
import numpy as np
from jax import lax
from jax.experimental import pallas as pl
from jax.experimental.pallas import tpu as pltpu

F32 = jnp.float32
BF16 = jnp.bfloat16

D_MODEL = 1024
N_META = 16
CHUNK = 128
PAD = CHUNK - N_META
RET_HEADS = 4
RET_QK_DIM = 128
RET_V_DIM = 256
RET_QK = RET_HEADS * RET_QK_DIM
RET_V = RET_HEADS * RET_V_DIM
SSD_D_INNER = 2 * D_MODEL
SSD_HEAD_DIM = 64
SSD_HEADS = SSD_D_INNER // SSD_HEAD_DIM
SSD_GROUPS = 4
SSD_HPG = SSD_HEADS // SSD_GROUPS
SSD_STATE = 128
SSD_GW = SSD_HPG * SSD_HEAD_DIM
SSD_BC = SSD_GROUPS * SSD_STATE
SSD_XBC = SSD_D_INNER + 2 * SSD_BC
D_FF = 2816
EPS = 1e-6
ROPE_BASE = 10000.0
LOG_GAMMA = tuple(math.log(1.0 - 2.0 ** (-5.0 - h)) for h in range(RET_HEADS))
LOG2E = math.log2(math.e)

COL_Z = 0
COL_XS = 2048
COL_B = 4096
COL_Q = 5120
COL_K = 5632
COL_V = 6144
COL_G = 7168
COL_GATES = 8192
PROJ_W = 10240
PROJ_TN = 1024
DT_W = 128

HALO = 16
LANES = 128
VMEM_LIMIT = 56 * 1024 * 1024


def _sigmoid(x):
    return 0.5 + 0.5 * jnp.tanh(0.5 * x)


def _silu(x):
    hx = 0.5 * x
    return hx + hx * jnp.tanh(hx)


def _softplus(x):
    return jnp.maximum(x, 0.0) + jnp.log1p(jnp.exp(-jnp.abs(x)))


def _split3(x):
    hi = x.astype(BF16)
    r1 = x - hi.astype(F32)
    mid = r1.astype(BF16)
    lo = (r1 - mid.astype(F32)).astype(BF16)
    return hi, mid, lo


def _dot(a, b):
    return jnp.dot(a, b, preferred_element_type=F32)


def _dot_tn(a, b):
    return lax.dot_general(a, b, (((0,), (0,)), ((), ())), preferred_element_type=F32)


def _dot_nt(a, b):
    return lax.dot_general(a, b, (((1,), (1,)), ((), ())), preferred_element_type=F32)


def _row_tile(rows, cap):
    best = CHUNK
    t = CHUNK
    while t <= min(rows, cap):
        if rows % t == 0:
            best = t
        t += CHUNK
    return best


def _rms(x, w):
    ms = jnp.mean(x * x, axis=-1, keepdims=True)
    return x * lax.rsqrt(ms + EPS) * w


def _conv_rows(f, w_ref, b_ref, cols, tm):
    prev = pltpu.roll(f, 1, 0)[HALO:HALO + tm]
    nxt = pltpu.roll(f, f.shape[0] - 1, 0)[HALO:HALO + tm]
    return (w_ref[0:1, cols] * prev + w_ref[1:2, cols] * f[HALO:HALO + tm] + w_ref[2:3, cols] * nxt
            + b_ref[:, cols])


_W_MAIN_COLS = 2 * RET_QK + 2 * RET_V + SSD_D_INNER + SSD_XBC


def _weight_source(c0):
    if c0 < COL_Q:
        return 0, c0 + 2 * RET_QK + 2 * RET_V
    if c0 < COL_GATES:
        return 0, c0 - COL_Q
    return 1, c0 - COL_GATES


def _proj_kernel(*refs, tm, n_tail_x, s_len):
    x_ref = refs[0]
    xc_refs = refs[1:1 + n_tail_x]
    (tail_ref, xp_ref, xn_ref, nw_ref, wa_ref, wg_ref, wdt_ref, cos_ref, sin_ref, valid_ref, cw_ref, cb_ref,
     o_ref, odt_ref, hm_ref, u_ref) = refs[1 + n_tail_x:]
    t = pl.program_id(1)
    is_last = t == pl.num_programs(1) - 1
    nw = nw_ref[...]

    tail = tail_ref[...]
    last_rows = jnp.concatenate([xc_ref[0] for xc_ref in xc_refs] + [tail], axis=0)
    rows = jnp.where(is_last, last_rows, x_ref[0])
    hm_ref[0] = rows
    prev = jnp.where(t == 0, tail[CHUNK - HALO:CHUNK, :], xp_ref[0])
    nxt = jnp.where((t + 1) * tm == s_len, tail[0:HALO, :], xn_ref[0])
    u_ref[0:HALO, :] = _rms(prev, nw).astype(BF16)
    u_ref[HALO:HALO + tm, :] = _rms(rows, nw).astype(BF16)
    u_ref[HALO + tm:2 * HALO + tm, :] = _rms(nxt, nw).astype(BF16)

    u = u_ref[HALO:HALO + tm, :]
    odt_ref[0] = _dot(u, wdt_ref[...])
    cos = cos_ref[...]
    sin = sin_ref[...]
    valid = valid_ref[...]
    w_refs = (wa_ref, wg_ref)
    for c0 in range(0, PROJ_W, PROJ_TN):
        src, s0 = _weight_source(c0)
        w = w_refs[src][:, s0:s0 + PROJ_TN]
        if COL_XS <= c0 < COL_Q:
            f = _dot(u_ref[...], w)
            x0 = c0 - COL_XS
            act = _silu(_conv_rows(f, cw_ref, cb_ref, slice(x0, x0 + PROJ_TN), tm))
            for k in range(PROJ_TN // LANES):
                o_ref[0, :, c0 + k * LANES:c0 + (k + 1) * LANES] = (
                    act[:, k * LANES:(k + 1) * LANES] * valid).astype(BF16)
        elif COL_Q <= c0 < COL_V:
            acc = _dot(u, w)
            for hb in range(PROJ_TN // LANES):
                a = acc[:, hb * LANES:(hb + 1) * LANES]
                r = a * cos + pltpu.roll(a, RET_QK_DIM // 2, 1) * sin
                if c0 + hb * LANES >= COL_K:
                    r = r * (RET_QK_DIM ** -0.5)
                o_ref[0, :, c0 + hb * LANES:c0 + (hb + 1) * LANES] = r.astype(BF16)
        else:
            o_ref[0, :, c0:c0 + PROJ_TN] = _dot(u, w).astype(BF16)


def _proj_call(x, tail, nw, wa, wg, wdt, cos, sin, valid, cw, cb):
    nb, s_len, _ = x.shape
    lm = s_len + CHUNK
    tm = _row_tile(lm, 384)
    nt = lm // tm
    hpt = tm // HALO
    n_tail_x = tm // CHUNK - 1
    n_xh = s_len // HALO

    def resident(shape):
        return pl.BlockSpec(shape, lambda b, t: (0,) * len(shape), pipeline_mode=pl.Buffered(1))

    tail_x_specs = [
        pl.BlockSpec((1, CHUNK, D_MODEL), lambda b, t, i=i: (b, s_len // CHUNK - n_tail_x + i, 0))
        for i in range(n_tail_x)]

    return pl.pallas_call(
        functools.partial(_proj_kernel, tm=tm, n_tail_x=n_tail_x, s_len=s_len),
        grid=(nb, nt),
        in_specs=[pl.BlockSpec((1, tm, D_MODEL), lambda b, t: (b, jnp.minimum(t, nt - 2), 0))] + tail_x_specs + [
            resident((CHUNK, D_MODEL)),
            pl.BlockSpec((1, HALO, D_MODEL), lambda b, t: (b, jnp.maximum(t * hpt - 1, 0), 0)),
            pl.BlockSpec((1, HALO, D_MODEL),
                         lambda b, t: (b, jnp.where(t == nt - 1, 0, jnp.minimum((t + 1) * hpt, n_xh - 1)), 0)),
            resident((1, D_MODEL)),
            resident((D_MODEL, _W_MAIN_COLS)),
            resident((D_MODEL, 2 * D_MODEL)),
            resident((D_MODEL, DT_W)),
            pl.BlockSpec((tm, LANES), lambda b, t: (t, 0)),
            pl.BlockSpec((tm, LANES), lambda b, t: (t, 0)),
            pl.BlockSpec((tm, LANES), lambda b, t: (t, 0)),
            resident((3, SSD_XBC)),
            resident((1, SSD_XBC)),
        ],
        out_specs=[
            pl.BlockSpec((1, tm, PROJ_W), lambda b, t: (b, t, 0)),
            pl.BlockSpec((1, tm, DT_W), lambda b, t: (b, t, 0)),
            pl.BlockSpec((1, tm, D_MODEL), lambda b, t: (b, t, 0)),
        ],
        out_shape=[
            jax.ShapeDtypeStruct((nb, lm, PROJ_W), BF16),
            jax.ShapeDtypeStruct((nb, lm, DT_W), F32),
            jax.ShapeDtypeStruct((nb, lm, D_MODEL), F32),
        ],
        scratch_shapes=[pltpu.VMEM((tm + 2 * HALO, D_MODEL), BF16)],
        compiler_params=pltpu.CompilerParams(
            dimension_semantics=("parallel", "parallel"), vmem_limit_bytes=VMEM_LIMIT),
        name="in_proj",
    )(x, *([x] * n_tail_x), tail, x, x, nw, wa, wg, wdt, cos, sin, valid, cw, cb)


def _chunk_pos():
    return lax.broadcasted_iota(jnp.int32, (CHUNK, 1), 0).astype(F32)


def _ret_bstate_chunk(k_ref, v_ref, o_ref, r_ref):
    pos = _chunk_pos()
    for h in range(RET_HEADS):
        lg = LOG_GAMMA[h]
        r = r_ref[h]
        o_ref[h * RET_QK_DIM:(h + 1) * RET_QK_DIM, :] = r.astype(BF16)
        k = k_ref[:, h * RET_QK_DIM:(h + 1) * RET_QK_DIM].astype(F32)
        kd = (k * jnp.exp(lg * pos)).astype(BF16)
        v = v_ref[:, h * RET_V_DIM:(h + 1) * RET_V_DIM]
        r_ref[h] = math.exp(CHUNK * lg) * r + _dot_tn(kd, v)


def _ret_main_chunk(q_ref, k_ref, v_ref, g_ref, rb_ref, gnw_ref, o_ref, r_ref):
    pos = _chunk_pos()
    li = lax.broadcasted_iota(jnp.int32, (CHUNK, CHUNK), 0)
    si = lax.broadcasted_iota(jnp.int32, (CHUNK, CHUNK), 1)
    dist = jnp.abs(li - si).astype(F32)
    for h in range(RET_HEADS):
        lg = LOG_GAMMA[h]
        q = q_ref[:, h * RET_QK_DIM:(h + 1) * RET_QK_DIM]
        k = k_ref[:, h * RET_QK_DIM:(h + 1) * RET_QK_DIM]
        v = v_ref[:, h * RET_V_DIM:(h + 1) * RET_V_DIM]
        r = r_ref[h]
        s = _dot_nt(q, k) * jnp.exp(lg * dist)
        rb = rb_ref[h * RET_QK_DIM:(h + 1) * RET_QK_DIM, :]
        qf = q.astype(F32)
        lhs = jnp.concatenate([s.astype(BF16), (qf * jnp.exp(lg * (pos + 1.0))).astype(BF16),
                               (qf * jnp.exp(lg * (CHUNK - pos))).astype(BF16)], axis=1)
        y = _dot(lhs, jnp.concatenate([v, r.astype(BF16), rb], axis=0))
        kd = (k.astype(F32) * jnp.exp(lg * (CHUNK - 1.0 - pos))).astype(BF16)
        r_ref[h] = math.exp(CHUNK * lg) * r + _dot_tn(kd, v)
        mu = jnp.mean(y, axis=-1, keepdims=True)
        yc = y - mu
        var = jnp.mean(yc * yc, axis=-1, keepdims=True)
        yn = yc * lax.rsqrt(var + EPS)
        sl = slice(h * RET_V_DIM, (h + 1) * RET_V_DIM)
        g = g_ref[:, sl].astype(F32)
        o_ref[:, sl] = (_silu(g) * (yn * gnw_ref[:, sl])).astype(BF16)


def _mem_block(c, nr):
    return jnp.where(c == 0, nr, c - 1)


def _valid_rows(is_meta_chunk):
    row = lax.broadcasted_iota(jnp.int32, (CHUNK, 1), 0)
    return jnp.where(jnp.logical_and(is_meta_chunk, row < PAD), 0.0, 1.0).astype(F32)


def _ssd_decays(dt_ref, dtb_ref, alog_ref, valid):
    dt = _softplus(dt_ref[...] + dtb_ref[...]) * valid
    da = dt * (-LOG2E * jnp.exp(alog_ref[...]))
    li = lax.broadcasted_iota(jnp.int32, (CHUNK, CHUNK), 0)
    si = lax.broadcasted_iota(jnp.int32, (CHUNK, CHUNK), 1)
    tri = jnp.where(si <= li, 1.0, 0.0).astype(BF16)
    hi, mid, lo = _split3(da)
    acs = _dot(tri, hi) + _dot(tri, mid) + _dot(tri, lo)
    return dt, da, acs


def _pack_hi_lo(x):
    lane = lax.broadcasted_iota(jnp.int32, x.shape, 1)
    hi = x.astype(BF16).astype(F32)
    return jnp.where(lane < LANES // 2, hi, pltpu.roll(x - hi, LANES // 2, 1)).astype(BF16)


def _ssd_bstate_chunk(xs_ref, b_ref, dt_ref, dtb_ref, alog_ref, eb_ref, o_ref, s_ref, valid):
    dt, da, acs = _ssd_decays(dt_ref, dtb_ref, alog_ref, valid)
    stack = jnp.concatenate([dt * jnp.exp2(acs - da), jnp.exp2(acs[CHUNK - HALO:CHUNK, :])], axis=0)
    ex = _dot(_pack_hi_lo(stack), eb_ref[...])
    xw = (xs_ref[...].astype(F32) * ex[0:CHUNK, :]).astype(BF16)
    cdec = ex[CHUNK + HALO - 1:CHUNK + HALO, :]
    bm = b_ref[...]
    for g in range(SSD_GROUPS):
        sl = slice(g * SSD_GW, (g + 1) * SSD_GW)
        s = s_ref[g]
        o_ref[g * SSD_STATE:(g + 1) * SSD_STATE, :] = s.astype(BF16)
        s_ref[g] = cdec[:, sl] * s + _dot_tn(bm[:, g * SSD_STATE:(g + 1) * SSD_STATE], xw[:, sl])


def _pack3(x, src):
    lane = lax.broadcasted_iota(jnp.int32, x.shape, 1)
    hi = x.astype(BF16).astype(F32)
    r1 = x - hi
    mid = r1.astype(BF16).astype(F32)
    lo = r1 - mid
    packed = jnp.where(lane < src + SSD_HEADS, hi,
                       jnp.where(lane < src + 2 * SSD_HEADS, pltpu.roll(mid, SSD_HEADS, 1),
                                 pltpu.roll(lo, 2 * SSD_HEADS, 1)))
    return packed.astype(BF16)


def _ssd_main_chunk(z_ref, xs_ref, bc_ref, dt_ref, sb_ref, dtb_ref, alog_ref, ef_ref, cf_ref,
                    dskip_ref, nw_ref, o_ref, s_ref, valid):
    dt, da, acs = _ssd_decays(dt_ref, dtb_ref, alog_ref, valid)
    last = acs[CHUNK - 1:CHUNK, :]
    to_right = last - (acs - da)
    ldt_t = jnp.log2(dt).T
    row_f = acs.T - ldt_t
    row_b = ldt_t - to_right.T
    diag_t = jnp.log2(dt + pltpu.roll(dt, LANES - SSD_HEADS, 1)).T

    li = lax.broadcasted_iota(jnp.int32, (CHUNK, CHUNK), 0)
    si = lax.broadcasted_iota(jnp.int32, (CHUNK, CHUNK), 1)
    below = si < li
    above = si > li
    lane = lax.broadcasted_iota(jnp.int32, (CHUNK, LANES), 1)
    first_half = lane < SSD_HEAD_DIM
    zero = jnp.zeros((CHUNK, LANES), BF16)
    acs_parts = _pack3(acs, 0)
    src_w = _pack_hi_lo(dt * jnp.exp2(last - acs))
    tiles = SSD_GW // LANES

    for g in range(SSD_GROUPS):
        sl = slice(g * SSD_GW, (g + 1) * SSD_GW)
        bg = bc_ref[:, g * SSD_STATE:(g + 1) * SSD_STATE]
        cg = bc_ref[:, SSD_BC + g * SSD_STATE:SSD_BC + (g + 1) * SSD_STATE]
        cb = _dot_nt(cg, bg)
        col_f = _dot(acs_parts, cf_ref[:, g * SSD_HPG * LANES:(g + 1) * SSD_HPG * LANES])

        def mix(h, causal_col, anti_col, cb=cb):
            lb = SSD_HEADS + h
            causal = causal_col - row_f[h:h + 1, :]
            anti = row_b[lb:lb + 1, :] + anti_col
            e = jnp.where(below, causal, jnp.where(above, anti, diag_t[h:h + 1, :]))
            return (cb * jnp.exp2(e)).astype(BF16)

        ea_f, ea_b, y_mix = [], [], []
        for pp in range(tiles):
            ha = g * SSD_HPG + 2 * pp
            hb = ha + 1
            fa = col_f[:, 2 * pp * LANES:(2 * pp + 1) * LANES]
            fb = col_f[:, (2 * pp + 1) * LANES:(2 * pp + 2) * LANES]
            ra = jnp.broadcast_to(to_right[:, SSD_HEADS + ha:SSD_HEADS + ha + 1], (CHUNK, LANES))
            rb = jnp.broadcast_to(to_right[:, SSD_HEADS + hb:SSD_HEADS + hb + 1], (CHUNK, LANES))
            ea_f.append(jnp.exp2(jnp.where(first_half, fa, fb)))
            ea_b.append(jnp.exp2(jnp.where(first_half, ra, rb)))
            xp = xs_ref[:, (g * tiles + pp) * LANES:(g * tiles + pp + 1) * LANES]
            rhs = jnp.concatenate([jnp.where(first_half, xp, zero), jnp.where(first_half, zero, xp)], axis=0)
            y_mix.append(_dot(jnp.concatenate([mix(ha, fa, ra), mix(hb, fb, rb)], axis=1), rhs))
        ea_fg = jnp.concatenate(ea_f, axis=1)
        ea_bg = jnp.concatenate(ea_b, axis=1)

        xs = xs_ref[:, sl].astype(F32)
        xw = (xs * _dot(src_w, ef_ref[:, sl])).astype(BF16)
        s = s_ref[g]
        y_off = ea_fg * _dot(cg, s.astype(BF16))
        y_off = y_off + ea_bg * _dot(cg, sb_ref[g * SSD_STATE:(g + 1) * SSD_STATE, :])
        s_ref[g] = ea_fg[CHUNK - 1:CHUNK, :] * s + _dot_tn(bg, xw)
        y = jnp.concatenate(y_mix, axis=1) + y_off + xs * dskip_ref[:, sl]
        y = y * _silu(z_ref[:, sl].astype(F32))
        o_ref[:, sl] = _rms(y, nw_ref[:, sl]).astype(BF16)


def _head_expander(offset):
    e = np.zeros((LANES, SSD_D_INNER), np.float32)
    for h in range(SSD_HEADS):
        e[offset + h, h * SSD_HEAD_DIM:(h + 1) * SSD_HEAD_DIM] = 1.0
        e[LANES // 2 + offset + h, h * SSD_HEAD_DIM:(h + 1) * SSD_HEAD_DIM] = 1.0
    return jnp.asarray(e, BF16)


def _lane_broadcaster(src):
    e = np.zeros((LANES, SSD_HEADS * LANES), np.float32)
    for h in range(SSD_HEADS):
        for part in range(3):
            e[src + part * SSD_HEADS + h, h * LANES:(h + 1) * LANES] = 1.0
    return jnp.asarray(e, BF16)


def _bstate_kernel(*refs, g_chunks):
    n_in = 5 * g_chunks
    dtb_ref, alog_ref, eb_ref, rb_ref, sb_ref, r_ref, s_ref = refs[n_in:]
    step = pl.program_id(1)

    @pl.when(step == 0)
    def _():
        r_ref[...] = jnp.zeros_like(r_ref)
        s_ref[...] = jnp.zeros_like(s_ref)

    is_last = step == pl.num_programs(1) - 1
    for p in range(g_chunks):
        k_ref, v_ref, xs_ref, b_ref, dt_ref = refs[5 * p:5 * p + 5]
        slot = g_chunks - 1 - p
        _ret_bstate_chunk(k_ref.at[0], v_ref.at[0], rb_ref.at[0, slot], r_ref)
        valid = _valid_rows(is_last if slot == 0 else False)
        _ssd_bstate_chunk(xs_ref.at[0], b_ref.at[0], dt_ref.at[0], dtb_ref, alog_ref, eb_ref,
                          sb_ref.at[0, slot], s_ref, valid)


def _main_kernel(*refs, g_chunks):
    n_in = 8 * g_chunks
    rb_ref, sb_ref, gnw_ref, dtb_ref, alog_ref, ef_ref, cf_ref, dskip_ref, nw_ref = refs[n_in:n_in + 9]
    yr_refs = refs[n_in + 9:n_in + 9 + g_chunks]
    ys_refs = refs[n_in + 9 + g_chunks:n_in + 9 + 2 * g_chunks]
    r_ref, s_ref = refs[n_in + 9 + 2 * g_chunks:]
    step = pl.program_id(1)

    @pl.when(step == 0)
    def _():
        r_ref[...] = jnp.zeros_like(r_ref)
        s_ref[...] = jnp.zeros_like(s_ref)

    for p in range(g_chunks):
        q_ref, k_ref, v_ref, g_ref, z_ref, xs_ref, bc_ref, dt_ref = refs[8 * p:8 * p + 8]
        _ret_main_chunk(q_ref.at[0], k_ref.at[0], v_ref.at[0], g_ref.at[0], rb_ref.at[0, p], gnw_ref,
                        yr_refs[p].at[0, 0], r_ref)
        valid = _valid_rows(step == 0 if p == 0 else False)
        _ssd_main_chunk(z_ref.at[0], xs_ref.at[0], bc_ref.at[0], dt_ref.at[0], sb_ref.at[0, p], dtb_ref, alog_ref,
                        ef_ref, cf_ref, dskip_ref, nw_ref, ys_refs[p].at[0, 0], s_ref, valid)


def _chunks_per_step(nc):
    return 3 if nc % 3 == 0 else 1


def _mixers(proj3, dt3, gnw, dtbias, alog, dskip, nw, nb, nc):
    nr = nc - 1
    gc = _chunks_per_step(nc)
    ng = nc // gc
    qb, vb = RET_QK, RET_V
    xw_, bw_, bcw_ = SSD_D_INNER, SSD_BC, 2 * SSD_BC
    e_f = _head_expander(0)
    e_b = _head_expander(SSD_HEADS)
    c_f = _lane_broadcaster(0)
    params = pltpu.CompilerParams(dimension_semantics=("parallel", "arbitrary"), vmem_limit_bytes=VMEM_LIMIT)

    def const(shape):
        return pl.BlockSpec(shape, lambda b, s: (0,) * len(shape))

    def chunk_spec(width, col, chunk_of):
        return pl.BlockSpec((1, CHUNK, width), lambda b, s: (b, _mem_block(chunk_of(s), nr), col // width))

    bwd_specs, bwd_args = [], []
    for p in range(gc):
        def chunk_of(s, p=p):
            return nc - 1 - (gc * s + p)
        bwd_specs += [chunk_spec(qb, COL_K, chunk_of), chunk_spec(vb, COL_V, chunk_of),
                      chunk_spec(xw_, COL_XS, chunk_of), chunk_spec(bw_, COL_B, chunk_of),
                      pl.BlockSpec((1, CHUNK, DT_W), lambda b, s, f=chunk_of: (b, _mem_block(f(s), nr), 0))]
        bwd_args += [proj3, proj3, proj3, proj3, dt3]

    rb, sb = pl.pallas_call(
        functools.partial(_bstate_kernel, g_chunks=gc),
        grid=(nb, ng),
        in_specs=bwd_specs + [const((1, DT_W)), const((1, DT_W)), const((LANES, SSD_D_INNER))],
        out_specs=[
            pl.BlockSpec((1, gc, RET_QK, RET_V_DIM), lambda b, s: (b, ng - 1 - s, 0, 0)),
            pl.BlockSpec((1, gc, SSD_BC, SSD_GW), lambda b, s: (b, ng - 1 - s, 0, 0)),
        ],
        out_shape=[
            jax.ShapeDtypeStruct((nb, nc, RET_QK, RET_V_DIM), BF16),
            jax.ShapeDtypeStruct((nb, nc, SSD_BC, SSD_GW), BF16),
        ],
        scratch_shapes=[
            pltpu.VMEM((RET_HEADS, RET_QK_DIM, RET_V_DIM), F32),
            pltpu.VMEM((SSD_GROUPS, SSD_STATE, SSD_GW), F32),
        ],
        compiler_params=params,
        name="mix_bwd_state",
    )(*bwd_args, dtbias, alog, e_b)

    fwd_specs, fwd_args = [], []
    for p in range(gc):
        def chunk_of(s, p=p):
            return gc * s + p
        fwd_specs += [chunk_spec(qb, COL_Q, chunk_of), chunk_spec(qb, COL_K, chunk_of),
                      chunk_spec(vb, COL_V, chunk_of), chunk_spec(vb, COL_G, chunk_of),
                      chunk_spec(xw_, COL_Z, chunk_of), chunk_spec(xw_, COL_XS, chunk_of),
                      chunk_spec(bcw_, COL_B, chunk_of),
                      pl.BlockSpec((1, CHUNK, DT_W), lambda b, s, f=chunk_of: (b, _mem_block(f(s), nr), 0))]
        fwd_args += [proj3] * 7 + [dt3]

    outs = pl.pallas_call(
        functools.partial(_main_kernel, g_chunks=gc),
        grid=(nb, ng),
        in_specs=fwd_specs + [
            pl.BlockSpec((1, gc, RET_QK, RET_V_DIM), lambda b, s: (b, s, 0, 0)),
            pl.BlockSpec((1, gc, SSD_BC, SSD_GW), lambda b, s: (b, s, 0, 0)),
            const((1, RET_V)), const((1, DT_W)), const((1, DT_W)),
            const((LANES, SSD_D_INNER)), const((LANES, SSD_HEADS * LANES)),
            const((1, SSD_D_INNER)), const((1, SSD_D_INNER)),
        ],
        out_specs=([pl.BlockSpec((1, 1, CHUNK, RET_V), lambda b, s: (b, s, 0, 0))] * gc
                   + [pl.BlockSpec((1, 1, CHUNK, SSD_D_INNER), lambda b, s: (b, s, 0, 0))] * gc),
        out_shape=([jax.ShapeDtypeStruct((nb, ng, CHUNK, RET_V), BF16)] * gc
                   + [jax.ShapeDtypeStruct((nb, ng, CHUNK, SSD_D_INNER), BF16)] * gc),
        scratch_shapes=[
            pltpu.VMEM((RET_HEADS, RET_QK_DIM, RET_V_DIM), F32),
            pltpu.VMEM((SSD_GROUPS, SSD_STATE, SSD_GW), F32),
        ],
        compiler_params=params,
        name="mix_main",
    )(*fwd_args, rb, sb, gnw, dtbias, alog, e_f, c_f, dskip, nw)
    return list(outs[:gc]), list(outs[gc:])


def _merge_kernel(*refs, g_chunks):
    h_ref = refs[0]
    yr_refs = refs[1:1 + g_chunks]
    ys_refs = refs[1 + g_chunks:1 + 2 * g_chunks]
    gates_ref, wr_ref, ws_ref, wo_ref, o_ref = refs[1 + 2 * g_chunks:]
    yr_in = jnp.concatenate([r[0, 0] for r in yr_refs], axis=0)
    ys_in = jnp.concatenate([r[0, 0] for r in ys_refs], axis=0)
    yr = _dot(yr_in, wr_ref[...])
    ys = _dot(ys_in, ws_ref[...])
    gr = _sigmoid(gates_ref[0, :, :D_MODEL].astype(F32))
    gs = _sigmoid(gates_ref[0, :, D_MODEL:].astype(F32))
    merged = (gr * yr + gs * ys).astype(BF16)
    o_ref[0] = h_ref[0] + _dot(merged, wo_ref[...])


def _merge_call(hm3, yr_list, ys_list, proj3, wr, ws, wo):
    nb, lm, _ = hm3.shape
    gc = len(yr_list)
    ng = yr_list[0].shape[1]
    tm = gc * CHUNK
    gw = 2 * D_MODEL

    def mixer_specs(arrays, width):
        specs, args = [], []
        for q in range(gc):
            if q < gc - 1:
                specs.append(pl.BlockSpec((1, 1, CHUNK, width), lambda b, t: (b, t, 0, 0)))
            else:
                specs.append(pl.BlockSpec((1, 1, CHUNK, width), lambda b, t: (b, lax.rem(t + 1, ng), 0, 0)))
            args.append(arrays[(q + 1) % gc])
        return specs, args

    yr_specs, yr_args = mixer_specs(yr_list, RET_V)
    ys_specs, ys_args = mixer_specs(ys_list, SSD_D_INNER)

    def resident(shape):
        return pl.BlockSpec(shape, lambda b, t: (0,) * len(shape))

    return pl.pallas_call(
        functools.partial(_merge_kernel, g_chunks=gc),
        grid=(nb, lm // tm),
        in_specs=[pl.BlockSpec((1, tm, D_MODEL), lambda b, t: (b, t, 0))] + yr_specs + ys_specs + [
            pl.BlockSpec((1, tm, gw), lambda b, t: (b, t, COL_GATES // gw)),
            resident((RET_V, D_MODEL)), resident((SSD_D_INNER, D_MODEL)), resident((D_MODEL, D_MODEL)),
        ],
        out_specs=pl.BlockSpec((1, tm, D_MODEL), lambda b, t: (b, t, 0)),
        out_shape=jax.ShapeDtypeStruct((nb, lm, D_MODEL), F32),
        compiler_params=pltpu.CompilerParams(
            dimension_semantics=("parallel", "parallel"), vmem_limit_bytes=VMEM_LIMIT),
        name="merge_out",
    )(hm3, *yr_args, *ys_args, proj3, wr, ws, wo)


def _ffn_kernel(h_ref, hp_ref, hn_ref, nw_ref, wup_ref, cw_ref, cb_ref, wd_ref, fnw_ref, o_ref, u_ref, act_ref,
                *, tm, tn):
    nw = nw_ref[...]
    u_ref[0:HALO, :] = _rms(hp_ref[0], nw).astype(BF16)
    u_ref[HALO:HALO + tm, :] = _rms(h_ref[0], nw).astype(BF16)
    u_ref[HALO + tm:2 * HALO + tm, :] = _rms(hn_ref[0], nw).astype(BF16)
    u = u_ref[...]
    nblk = D_FF // tn
    half = (nblk + 1) // 2

    def up(j):
        return (_dot(u, wup_ref[:, j * tn:(j + 1) * tn]),
                _dot(u, wup_ref[:, D_FF + j * tn:D_FF + (j + 1) * tn]))

    nxt = up(0)
    for j in range(nblk):
        fg, fu = nxt
        if j + 1 < nblk:
            nxt = up(j + 1)
        gc = slice(j * tn, (j + 1) * tn)
        uc = slice(D_FF + j * tn, D_FF + (j + 1) * tn)
        act = _silu(_conv_rows(fg, cw_ref, cb_ref, gc, tm)) * _conv_rows(fu, cw_ref, cb_ref, uc, tm)
        act_ref[:, gc] = act.astype(BF16)
        if j == half - 1:
            y_a = _dot(act_ref[:, :half * tn], wd_ref[:half * tn, :])
    y = h_ref[0] + y_a + _dot(act_ref[:, half * tn:], wd_ref[half * tn:, :])
    o_ref[0] = _rms(y, fnw_ref[...])


def _ffn_call(hmid3, nw, w_up, cw, cb, w_down, fnw, nb, s_len):
    lm = hmid3.shape[1]
    tm = 512 if s_len % 512 == 0 else CHUNK
    tn = 256
    nh = lm // HALO
    hpt = tm // HALO

    def prev_blk(t):
        return jnp.where(t == 0, nh - 1, t * hpt - 1)

    def resident(shape):
        return pl.BlockSpec(shape, lambda b, t: (0,) * len(shape), pipeline_mode=pl.Buffered(1))

    return pl.pallas_call(
        functools.partial(_ffn_kernel, tm=tm, tn=tn),
        grid=(nb, s_len // tm),
        in_specs=[
            pl.BlockSpec((1, tm, D_MODEL), lambda b, t: (b, t, 0)),
            pl.BlockSpec((1, HALO, D_MODEL), lambda b, t: (b, prev_blk(t), 0)),
            pl.BlockSpec((1, HALO, D_MODEL), lambda b, t: (b, (t + 1) * hpt, 0)),
            resident((1, D_MODEL)),
            resident((D_MODEL, 2 * D_FF)),
            resident((3, 2 * D_FF)),
            resident((1, 2 * D_FF)),
            resident((D_FF, D_MODEL)),
            resident((1, D_MODEL)),
        ],
        out_specs=pl.BlockSpec((1, tm, D_MODEL), lambda b, t: (b, t, 0)),
        out_shape=jax.ShapeDtypeStruct((nb, s_len, D_MODEL), F32),
        scratch_shapes=[pltpu.VMEM((tm + 2 * HALO, D_MODEL), BF16), pltpu.VMEM((tm, D_FF), BF16)],
        compiler_params=pltpu.CompilerParams(
            dimension_semantics=("parallel", "parallel"), vmem_limit_bytes=VMEM_LIMIT),
        name="ffn",
    )(hmid3, hmid3, hmid3, nw, w_up, cw, cb, w_down, fnw)


def _position_tables(s_len):
    half = RET_QK_DIM // 2
    inv = ROPE_BASE ** (-jnp.arange(half, dtype=F32) / half)
    pos = jnp.concatenate([N_META + jnp.arange(s_len), jnp.zeros((PAD,), jnp.int32), jnp.arange(N_META)])
    ang = pos.astype(F32)[:, None] * inv[None, :]
    cos, sin = jnp.cos(ang), jnp.sin(ang)
    valid = jnp.concatenate([jnp.ones((s_len,), F32), jnp.zeros((PAD,), F32), jnp.ones((N_META,), F32)])
    return (jnp.concatenate([cos, cos], axis=1), jnp.concatenate([-sin, sin], axis=1),
            jnp.broadcast_to(valid[:, None], (s_len + CHUNK, LANES)))


def kernel(x, meta_tokens, norm_mix_w, w_in, ret_gn_w, w_ret_out, w_ssd_conv, b_ssd_conv, dt_bias_f, dt_bias_b, a_log_f, a_log_b, d_skip, ssd_norm_w, w_ssd_out, w_out, norm_ffn_w, w_ffn_up, w_ffn_conv, b_ffn_conv, w_ffn_down, final_norm_w):
    assert norm_mix_w.shape[0] == 1, "single-layer block"
    nb, s_len, _ = x.shape
    assert s_len % CHUNK == 0
    lm = s_len + CHUNK
    nc = lm // CHUNK

    tail = jnp.concatenate([jnp.zeros((PAD, D_MODEL), x.dtype), meta_tokens.astype(x.dtype)], axis=0)

    wi = w_in[0]
    dt0 = _W_MAIN_COLS
    w_a = wi[:, :dt0].astype(BF16)
    w_g = wi[:, dt0 + 2 * SSD_HEADS:].astype(BF16)
    w_dt = jnp.pad(wi[:, dt0:dt0 + 2 * SSD_HEADS], ((0, 0), (0, DT_W - 2 * SSD_HEADS))).astype(BF16)
    cos, sin, valid = _position_tables(s_len)

    proj3, dt3, hm = _proj_call(x, tail, norm_mix_w, w_a, w_g, w_dt, cos, sin, valid, w_ssd_conv[0], b_ssd_conv)

    lane_pad = jnp.zeros((1, DT_W - 2 * SSD_HEADS), F32)
    dtbias = jnp.concatenate([dt_bias_f, dt_bias_b, lane_pad], axis=1)
    alog = jnp.concatenate([a_log_f, a_log_b, lane_pad], axis=1)
    dskip = jnp.repeat(d_skip, SSD_HEAD_DIM, axis=1)
    y_ret, y_ssd = _mixers(proj3, dt3, ret_gn_w, dtbias, alog, dskip, ssd_norm_w, nb, nc)

    h_mid = _merge_call(hm, y_ret, y_ssd, proj3, w_ret_out[0].astype(BF16), w_ssd_out[0].astype(BF16),
                        w_out[0].astype(BF16))

    return _ffn_call(h_mid, norm_ffn_w, w_ffn_up[0].astype(BF16), w_ffn_conv[0],
                     b_ffn_conv, w_ffn_down[0].astype(BF16), final_norm_w.reshape(1, D_MODEL), nb, s_len)
```
